```python
import jax, jax.numpy as jnp
from jax import lax

D_MODEL = 2048
BATCH = 4
SEQ = 4096
DEPTH = 2

CHUNK = 64
Q_BLOCK = 128
ROPE_BASE = 10000.0
EPS = 1e-6
F32 = jnp.float32

MLA_HEADS = 8
MLA_Q_RANK = 512
MLA_KV_RANK = 256
MLA_NOPE = 128
MLA_ROPE = 64
MLA_V = 128
RET_HEADS = 8
RET_DK = 128
RET_DV = 128
MIX_WIDTH = MLA_HEADS * MLA_V + RET_HEADS * RET_DV

IN_SIZES = (MLA_Q_RANK, MLA_KV_RANK, MLA_ROPE,
            RET_HEADS * RET_DK, RET_HEADS * RET_DK, RET_HEADS * RET_DV, RET_HEADS * RET_DV)
IN_WIDTH = sum(IN_SIZES)
IN_OFFSETS = tuple(sum(IN_SIZES[:i + 1]) for i in range(len(IN_SIZES) - 1))

N_GROUPS = 4
EXPERTS_PER_GROUP = 8
N_EXPERTS = N_GROUPS * EXPERTS_PER_GROUP
TOP_K = 2
D_EXPERT = 512

kernel_name = 'hybrid_mla_retention_hmoe_block'


def rms_norm(x, g):
    xf = x.astype(F32)
    y = xf * lax.rsqrt(jnp.mean(xf * xf, axis=-1, keepdims=True) + EPS)
    return (y * g.astype(F32)).astype(x.dtype)


def rope(x, pos):
    half = x.shape[-1] // 2
    inv = ROPE_BASE ** (-jnp.arange(half, dtype=F32) / half)
    ang = pos.astype(F32)[:, None] * inv[None, :]
    cos = jnp.cos(ang)[None, :, None, :]
    sin = jnp.sin(ang)[None, :, None, :]
    xf = x.astype(F32)
    x1, x2 = xf[..., :half], xf[..., half:]
    return jnp.concatenate([x1 * cos - x2 * sin, x1 * sin + x2 * cos], axis=-1).astype(x.dtype)


def chunk_block_attention(q, k, v):
    S = q.shape[1]
    scale = q.shape[-1] ** -0.5
    outs = []
    for j in range(S // Q_BLOCK):
        q0, q1 = j * Q_BLOCK, (j + 1) * Q_BLOCK
        qb, kb, vb = q[:, q0:q1], k[:, :q1], v[:, :q1]
        s = jnp.einsum('bqhd,bkhd->bhqk', qb, kb, preferred_element_type=F32) * scale
        q_chunk = (q0 + jnp.arange(Q_BLOCK)) // CHUNK
        k_chunk = jnp.arange(q1) // CHUNK
        s = jnp.where(k_chunk[None, :] <= q_chunk[:, None], s, -jnp.inf)
        p = jax.nn.softmax(s, axis=-1).astype(vb.dtype)
        outs.append(jnp.einsum('bhqk,bkhd->bqhd', p, vb))
    return jnp.concatenate(outs, axis=1)


def chunkwise_retention(q, k, v):
    B, S, H, dk = q.shape
    dv = v.shape[-1]
    N = S // CHUNK
    log_g = jnp.log1p(-jnp.exp2(-5.0 - jnp.arange(H, dtype=F32)))
    q = q.astype(F32).reshape(B, N, CHUNK, H, dk)
    k = (k.astype(F32) * dk ** -0.5).reshape(B, N, CHUNK, H, dk)
    v = v.astype(F32).reshape(B, N, CHUNK, H, dv)
    idx = jnp.arange(CHUNK, dtype=F32)
    rel = idx[:, None] - idx[None, :]
    decay = jnp.where(rel[None] >= 0, jnp.exp(jnp.maximum(rel, 0.0)[None] * log_g[:, None, None]), 0.0)
    scores = jnp.einsum('bnqhd,bnkhd->bnhqk', q, k) * decay[None, None]
    y_intra = jnp.einsum('bnhqk,bnkhe->bnqhe', scores, v)
    k_dec = k * jnp.exp((CHUNK - 1.0 - idx)[:, None] * log_g[None, :])[None, None, :, :, None]
    U = jnp.einsum('bnkhd,bnkhe->nbhde', k_dec, v)
    chunk_decay = jnp.exp(CHUNK * log_g)[None, :, None, None]

    def step(state, u):
        return state * chunk_decay + u, state

    _, S_before = lax.scan(step, jnp.zeros((B, H, dk, dv), F32), U)
    q_dec = q * jnp.exp((idx + 1.0)[:, None] * log_g[None, :])[None, None, :, :, None]
    y_inter = jnp.einsum('bnqhd,nbhde->bnqhe', q_dec, S_before)
    return (y_intra + y_inter).reshape(B, S, H, dv)


def head_group_norm(y):
    mu = jnp.mean(y, axis=-1, keepdims=True)
    var = jnp.mean(jnp.square(y - mu), axis=-1, keepdims=True)
    return (y - mu) * lax.rsqrt(var + EPS)


def hybrid_mixer(h, w_in, q_norm_g, w_uq, kv_norm_g, w_ukv, w_o):
    B, S, _ = h.shape
    pos = jnp.arange(S, dtype=jnp.int32)
    proj = h @ w_in
    c_q, c_kv, k_pe, q_r, k_r, v_r, g_r = jnp.split(proj, list(IN_OFFSETS), axis=-1)

    q = (rms_norm(c_q, q_norm_g) @ w_uq).reshape(B, S, MLA_HEADS, MLA_NOPE + MLA_ROPE)
    q_nope, q_pe = q[..., :MLA_NOPE], rope(q[..., MLA_NOPE:], pos)
    kv = (rms_norm(c_kv, kv_norm_g) @ w_ukv).reshape(B, S, MLA_HEADS, MLA_NOPE + MLA_V)
    k_nope, v_mla = kv[..., :MLA_NOPE], kv[..., MLA_NOPE:]
    k_pe = rope(k_pe[:, :, None, :], pos)
    q_mla = jnp.concatenate([q_nope, q_pe], axis=-1)
    k_mla = jnp.concatenate([k_nope, jnp.broadcast_to(k_pe, (B, S, MLA_HEADS, MLA_ROPE))], axis=-1)
    y_mla = chunk_block_attention(q_mla, k_mla, v_mla).reshape(B, S, MLA_HEADS * MLA_V)

    q_r = rope(q_r.reshape(B, S, RET_HEADS, RET_DK), pos)
    k_r = rope(k_r.reshape(B, S, RET_HEADS, RET_DK), pos)
    v_r = v_r.reshape(B, S, RET_HEADS, RET_DV)
    y_ret = head_group_norm(chunkwise_retention(q_r, k_r, v_r)).reshape(B, S, RET_HEADS * RET_DV)
    y_ret = (jax.nn.silu(g_r.astype(F32)) * y_ret).astype(h.dtype)

    return jnp.concatenate([y_mla.astype(h.dtype), y_ret], axis=-1) @ w_o


def hier_moe(h, wg, bg, we, be, w_gate, w_up, w_down):
    B, S, D = h.shape
    T = B * S
    t = h.reshape(T, D)
    g_logits = (t @ wg + bg).astype(F32)
    g_prob = jax.nn.softmax(g_logits, axis=-1)
    _, g_sel = lax.top_k(g_logits, 1)
    g_sel = g_sel[:, 0]
    g_onehot = jax.nn.one_hot(g_sel, N_GROUPS, dtype=F32)
    p_group = jnp.sum(g_prob * g_onehot, axis=-1, keepdims=True)
    e_logits = (t @ we + be).astype(F32).reshape(T, N_GROUPS, EXPERTS_PER_GROUP)
    e_in_group = jnp.einsum('tg,tge->te', g_onehot, e_logits)
    top_logit, top_idx = lax.top_k(e_in_group, TOP_K)
    top_w = jax.nn.softmax(top_logit, axis=-1) * p_group
    expert_id = g_sel[:, None] * EXPERTS_PER_GROUP + top_idx
    gates = jnp.sum(jax.nn.one_hot(expert_id, N_EXPERTS, dtype=F32) * top_w[..., None], axis=1).astype(t.dtype)
    out = jnp.zeros((T, D), t.dtype)
    for gi in range(N_GROUPS):
        e0, e1 = gi * EXPERTS_PER_GROUP, (gi + 1) * EXPERTS_PER_GROUP
        a = jnp.einsum('td,edf->tef', t, w_gate[e0:e1])
        u = jnp.einsum('td,edf->tef', t, w_up[e0:e1])
        hid = jax.nn.silu(a) * u * gates[:, e0:e1, None]
        out = out + jnp.einsum('tef,efd->td', hid, w_down[e0:e1])
    return out.reshape(B, S, D)


def setup_inputs(seed: int = 0) -> dict:
    key = jax.random.key(seed)
    ks = jax.random.split(key, 24)
    L, D = DEPTH, D_MODEL

    def nrm(k, shape, s):
        return jax.random.normal(k, shape, F32) * s

    return {
        'x': nrm(ks[0], (BATCH, SEQ, D), 1.0),
        'c': nrm(ks[1], (BATCH, D), 1.0),
        'ada_w': nrm(ks[2], (L, D, 6 * D), 0.5 * D ** -0.5),
        'ada_b': nrm(ks[3], (L, 6 * D), 0.02),
        'norm1_g': 1.0 + nrm(ks[4], (L, D), 0.02),
        'w_in': nrm(ks[5], (L, D, IN_WIDTH), D ** -0.5),
        'q_norm_g': 1.0 + nrm(ks[6], (L, MLA_Q_RANK), 0.02),
        'w_uq': nrm(ks[7], (L, MLA_Q_RANK, MLA_HEADS * (MLA_NOPE + MLA_ROPE)), MLA_Q_RANK ** -0.5),
        'kv_norm_g': 1.0 + nrm(ks[8], (L, MLA_KV_RANK), 0.02),
        'w_ukv': nrm(ks[9], (L, MLA_KV_RANK, MLA_HEADS * (MLA_NOPE + MLA_V)), MLA_KV_RANK ** -0.5),
        'w_o': nrm(ks[10], (L, MIX_WIDTH, D), MIX_WIDTH ** -0.5),
        'norm2_g': 1.0 + nrm(ks[11], (L, D), 0.02),
        'router_group_w': nrm(ks[12], (L, D, N_GROUPS), D ** -0.5),
        'router_group_b': nrm(ks[13], (L, N_GROUPS), 0.01),
        'router_expert_w': nrm(ks[14], (L, D, N_EXPERTS), D ** -0.5),
        'router_expert_b': nrm(ks[15], (L, N_EXPERTS), 0.01),
        'w_gate': nrm(ks[16], (L, N_EXPERTS, D, D_EXPERT), D ** -0.5),
        'w_up': nrm(ks[17], (L, N_EXPERTS, D, D_EXPERT), D ** -0.5),
        'w_down': nrm(ks[18], (L, N_EXPERTS, D_EXPERT, D), D_EXPERT ** -0.5),
        'final_norm_g': 1.0 + nrm(ks[19], (D,), 0.02),
    }


def reference(x, c, ada_w, ada_b, norm1_g, w_in, q_norm_g, w_uq, kv_norm_g, w_ukv, w_o,
              norm2_g, router_group_w, router_group_b, router_expert_w, router_expert_b,
              w_gate, w_up, w_down, final_norm_g):
    for l in range(DEPTH):
        mod = jax.nn.silu(c) @ ada_w[l] + ada_b[l]
        sh1, sc1, g1, sh2, sc2, g2 = [m[:, None, :] for m in jnp.split(mod, 6, axis=-1)]
        h = rms_norm(x, norm1_g[l]) * (1.0 + sc1) + sh1
        x = x + g1 * hybrid_mixer(h, w_in[l], q_norm_g[l], w_uq[l], kv_norm_g[l], w_ukv[l], w_o[l])
        h = rms_norm(x, norm2_g[l]) * (1.0 + sc2) + sh2
        x = x + g2 * hier_moe(h, router_group_w[l], router_group_b[l], router_expert_w[l],
                              router_expert_b[l], w_gate[l], w_up[l], w_down[l])
    return rms_norm(x, final_norm_g)
```

```python
import functools

import numpy as np
import jax
import jax.numpy as jnp
from jax import lax
from jax.experimental import pallas as pl
from jax.experimental.pallas import tpu as pltpu

F32 = jnp.float32
BF16 = jnp.bfloat16
I32 = jnp.int32
EPS = 1e-6
ROPE_BASE = 10000.0
CHUNK = 64

MLA_HEADS = 8
MLA_Q_RANK = 512
MLA_KV_RANK = 256
MLA_NOPE = 128
MLA_ROPE = 64
MLA_V = 128
RET_HEADS = 8
RET_DK = 128
RET_DV = 128
N_GROUPS = 4
EXPERTS_PER_GROUP = 8
N_EXPERTS = N_GROUPS * EXPERTS_PER_GROUP
CHUNK_SHIFT = CHUNK.bit_length() - 1
EPG_SHIFT = EXPERTS_PER_GROUP.bit_length() - 1

LANES = 128
QK_WIDTH = 2 * LANES
MIX_HALF = MLA_HEADS * MLA_V
PROJ_GROUP = 1024
PROJ_WIDTH = 5 * PROJ_GROUP
NEG_BIG = -1e30
VMEM_LIMIT = 56 * 1024 * 1024


def _tile(n, pref):
    if n <= pref:
        return n
    t = pref - pref % LANES
    while n % t:
        t -= LANES
    assert t > 0, (n, pref)
    return t


def _params(sem, vmem=VMEM_LIMIT):
    return pltpu.CompilerParams(dimension_semantics=sem, vmem_limit_bytes=vmem)


def _mod_kernel(c_ref, w_ref, b_ref, o_ref):
    c = c_ref[...]
    s = (c * jax.nn.sigmoid(c)).astype(BF16)
    o_ref[...] = jnp.dot(s, w_ref[...].astype(BF16), preferred_element_type=F32) + b_ref[...]


def _adaln_mod(c, ada_w, ada_b):
    B, D = c.shape
    L, _, N = ada_w.shape
    bp = 8
    cp = jnp.pad(c, ((0, bp - B), (0, 0)))
    tn = _tile(N, 1024)
    out = pl.pallas_call(
        _mod_kernel,
        grid=(L, N // tn),
        in_specs=[
            pl.BlockSpec((bp, D), lambda l, j: (0, 0)),
            pl.BlockSpec((None, D, tn), lambda l, j: (l, 0, j)),
            pl.BlockSpec((None, 1, tn), lambda l, j: (l, 0, j)),
        ],
        out_specs=pl.BlockSpec((None, bp, tn), lambda l, j: (l, 0, j)),
        out_shape=jax.ShapeDtypeStruct((L, bp, N), F32),
        compiler_params=_params(("arbitrary", "arbitrary")),
        name="adaln_mod",
    )(cp, ada_w, ada_b.reshape(L, 1, N))
    return out[:, :B]


def _norm_matmul_kernel(x_ref, a_ref, b_ref, w_ref, o_ref, h_ref):
    @pl.when(pl.program_id(2) == 0)
    def _():
        x = x_ref[...]
        ms = jnp.mean(x * x, axis=-1, keepdims=True)
        h_ref[...] = (x * lax.rsqrt(ms + EPS) * a_ref[...] + b_ref[...]).astype(BF16)

    o_ref[...] = jnp.dot(h_ref[...], w_ref[...], preferred_element_type=F32)


def _norm_matmul(x, a, b, w):
    B, S, D = x.shape
    N = w.shape[1]
    tm = _tile(S, 1024)
    tn = _tile(N, 640)
    return pl.pallas_call(
        _norm_matmul_kernel,
        grid=(B, S // tm, N // tn),
        in_specs=[
            pl.BlockSpec((None, tm, D), lambda b, i, j: (b, i, 0)),
            pl.BlockSpec((None, 1, D), lambda b, i, j: (b, 0, 0)),
            pl.BlockSpec((None, 1, D), lambda b, i, j: (b, 0, 0)),
            pl.BlockSpec((D, tn), lambda b, i, j: (0, j)),
        ],
        out_specs=pl.BlockSpec((None, tm, tn), lambda b, i, j: (b, i, j)),
        out_shape=jax.ShapeDtypeStruct((B, S, N), F32),
        scratch_shapes=[pltpu.VMEM((tm, D), BF16)],
        compiler_params=_params(("arbitrary", "arbitrary", "arbitrary")),
        name="norm_matmul",
    )(x, a, b, w)


def _rms(x, g):
    return x * lax.rsqrt(jnp.mean(x * x, axis=-1, keepdims=True) + EPS) * g


def _rope_block(x, cosb, sinb):
    return x * cosb + pltpu.roll(x, LANES // 2, 1) * sinb


def _mla_prep_kernel(p_ref, qg_ref, kvg_ref, wuq_ref, wukv_ref, cos_ref, sin_ref, q_ref, k_ref, v_ref):
    p = p_ref[...]
    cq = p[:, :MLA_Q_RANK]
    ckv = p[:, MLA_Q_RANK:MLA_Q_RANK + MLA_KV_RANK]
    kpe = p[:, MLA_Q_RANK + MLA_KV_RANK:MLA_Q_RANK + MLA_KV_RANK + LANES]
    cosb = cos_ref[...]
    sinb = sin_ref[...]
    scale = (MLA_NOPE + MLA_ROPE) ** -0.5
    qf = jnp.dot(_rms(cq, qg_ref[...]).astype(BF16), wuq_ref[...], preferred_element_type=F32)
    kvf = jnp.dot(_rms(ckv, kvg_ref[...]).astype(BF16), wukv_ref[...], preferred_element_type=F32)
    kpe_r = _rope_block(kpe, cosb, sinb).astype(BF16)
    for h in range(MLA_HEADS):
        c0 = h * QK_WIDTH
        q_ref[h, :, :LANES] = (qf[:, c0:c0 + LANES] * scale).astype(BF16)
        q_ref[h, :, LANES:] = (_rope_block(qf[:, c0 + LANES:c0 + QK_WIDTH], cosb, sinb) * scale).astype(BF16)
        k_ref[h, :, :LANES] = kvf[:, c0:c0 + LANES].astype(BF16)
        k_ref[h, :, LANES:] = kpe_r
        v_ref[h] = kvf[:, c0 + LANES:c0 + QK_WIDTH].astype(BF16)


def _mla_prep(proj, qg, kvg, wuq, wukv, cosb, sinb):
    B, S, _ = proj.shape
    H = MLA_HEADS
    tm = _tile(S, 512)
    return pl.pallas_call(
        _mla_prep_kernel,
        grid=(B, S // tm),
        in_specs=[
            pl.BlockSpec((None, tm, PROJ_GROUP), lambda b, i: (b, i, 0)),
            pl.BlockSpec((1, MLA_Q_RANK), lambda b, i: (0, 0)),
            pl.BlockSpec((1, MLA_KV_RANK), lambda b, i: (0, 0)),
            pl.BlockSpec((MLA_Q_RANK, H * QK_WIDTH), lambda b, i: (0, 0)),
            pl.BlockSpec((MLA_KV_RANK, H * QK_WIDTH), lambda b, i: (0, 0)),
            pl.BlockSpec((tm, LANES), lambda b, i: (i, 0)),
            pl.BlockSpec((tm, LANES), lambda b, i: (i, 0)),
        ],
        out_specs=[
            pl.BlockSpec((None, H, tm, QK_WIDTH), lambda b, i: (b, 0, i, 0)),
            pl.BlockSpec((None, H, tm, QK_WIDTH), lambda b, i: (b, 0, i, 0)),
            pl.BlockSpec((None, H, tm, MLA_V), lambda b, i: (b, 0, i, 0)),
        ],
        out_shape=[
            jax.ShapeDtypeStruct((B, H, S, QK_WIDTH), BF16),
            jax.ShapeDtypeStruct((B, H, S, QK_WIDTH), BF16),
            jax.ShapeDtypeStruct((B, H, S, MLA_V), BF16),
        ],
        compiler_params=_params(("arbitrary", "arbitrary")),
        name="mla_prep",
    )(proj, qg, kvg, wuq, wukv, cosb, sinb)


_NT = (((1,), (1,)), ((), ()))
_TN = (((0,), (0,)), ((), ()))


def _attn_kernel(q_ref, k_ref, v_ref, o_ref, *, tq, tk):
    i = pl.program_id(2)
    q = q_ref[...]
    sub = tq // tk

    def step(j, carry, masked_sub):
        m, l, acc = carry
        start = pl.multiple_of(j * tk, tk)
        k = k_ref[pl.ds(start, tk), :]
        v = v_ref[pl.ds(start, tk), :]
        s = lax.dot_general(q, k, _NT, preferred_element_type=F32)
        if masked_sub is not None:
            qc = jnp.right_shift(lax.broadcasted_iota(I32, (tq, tk), 0), CHUNK_SHIFT)
            kc = jnp.right_shift(lax.broadcasted_iota(I32, (tq, tk), 1) + masked_sub * tk, CHUNK_SHIFT)
            s = jnp.where(kc <= qc, s, NEG_BIG)
        m_new = jnp.maximum(m, jnp.max(s, axis=-1, keepdims=True))
        alpha = jnp.exp(m - m_new)
        p = jnp.exp(s - m_new)
        l = alpha * l + jnp.sum(p, axis=-1, keepdims=True)
        acc = alpha * acc + jnp.dot(p.astype(BF16), v, preferred_element_type=F32)
        return m_new, l, acc

    init = (jnp.full((tq, 1), NEG_BIG, F32), jnp.zeros((tq, 1), F32), jnp.zeros((tq, MLA_V), F32))
    carry = lax.fori_loop(0, i * sub, lambda j, c: step(j, c, None), init)
    for jj in range(sub):
        carry = step(i * sub + jj, carry, jj)
    _, l, acc = carry
    o_ref[...] = (acc / l).astype(o_ref.dtype)


def _attention(q, k, v):
    B, H, S, _ = q.shape
    tq = _tile(S, 256)
    tk = _tile(tq, 256)
    return pl.pallas_call(
        functools.partial(_attn_kernel, tq=tq, tk=tk),
        grid=(B, H, S // tq),
        in_specs=[
            pl.BlockSpec((None, None, tq, QK_WIDTH), lambda b, h, i: (b, h, i, 0)),
            pl.BlockSpec((None, None, S, QK_WIDTH), lambda b, h, i: (b, h, 0, 0)),
            pl.BlockSpec((None, None, S, MLA_V), lambda b, h, i: (b, h, 0, 0)),
        ],
        out_specs=pl.BlockSpec((None, tq, MLA_V), lambda b, h, i: (b, i, h)),
        out_shape=jax.ShapeDtypeStruct((B, S, H * MLA_V), BF16),
        compiler_params=_params(("arbitrary", "arbitrary", "arbitrary")),
        name="attention",
    )(q, k, v)


def _retention_kernel(q_ref, k_ref, v_ref, g_ref, cos_ref, sin_ref, o_ref, state_ref, decay_ref, *, tb):
    first_block = pl.program_id(1) == 0
    log_g = [float(np.log1p(-np.exp2(-5.0 - h))) for h in range(RET_HEADS)]

    @pl.when(jnp.logical_and(pl.program_id(0) == 0, first_block))
    def _():
        rel = (lax.broadcasted_iota(I32, (tb, tb), 0) - lax.broadcasted_iota(I32, (tb, tb), 1)).astype(F32)
        for h in range(RET_HEADS):
            decay_ref[h] = jnp.where(rel >= 0, jnp.exp(jnp.maximum(rel, 0.0) * log_g[h]), 0.0)

    @pl.when(first_block)
    def _():
        state_ref[...] = jnp.zeros_like(state_ref)

    cosr = cos_ref[...]
    sinr = sin_ref[...]
    idx = lax.broadcasted_iota(I32, (tb, 1), 0).astype(F32)
    for h in range(RET_HEADS):
        sl = slice(h * RET_DK, (h + 1) * RET_DK)
        q = _rope_block(q_ref[:, sl], cosr, sinr)
        k = _rope_block(k_ref[:, sl], cosr, sinr) * (RET_DK ** -0.5)
        vb = v_ref[:, sl].astype(BF16)
        scores = lax.dot_general(q.astype(BF16), k.astype(BF16), _NT, preferred_element_type=F32) * decay_ref[h]
        y = jnp.dot(scores.astype(BF16), vb, preferred_element_type=F32)
        state = state_ref[h]
        q_dec = q * jnp.exp((idx + 1.0) * log_g[h])
        y = y + jnp.dot(q_dec.astype(BF16), state.astype(BF16), preferred_element_type=F32)
        k_dec = k * jnp.exp((tb - 1.0 - idx) * log_g[h])
        upd = lax.dot_general(k_dec.astype(BF16), vb, _TN, preferred_element_type=F32)
        state_ref[h] = state * float(np.exp(tb * log_g[h])) + upd
        mu = jnp.mean(y, axis=-1, keepdims=True)
        yc = y - mu
        var = jnp.mean(yc * yc, axis=-1, keepdims=True)
        g = g_ref[:, sl]
        o_ref[:, sl] = (g * jax.nn.sigmoid(g) * (yc * lax.rsqrt(var + EPS))).astype(o_ref.dtype)


def _retention(proj, cosr, sinr):
    B, S, _ = proj.shape
    tb = _tile(S, 256)
    W = RET_HEADS * RET_DK
    col = lambda g: pl.BlockSpec((None, tb, W), lambda b, i: (b, i, g))
    return pl.pallas_call(
        functools.partial(_retention_kernel, tb=tb),
        grid=(B, S // tb),
        in_specs=[col(1), col(2), col(3), col(4),
                  pl.BlockSpec((tb, LANES), lambda b, i: (i, 0)),
                  pl.BlockSpec((tb, LANES), lambda b, i: (i, 0))],
        out_specs=pl.BlockSpec((None, tb, W), lambda b, i: (b, i, 0)),
        out_shape=jax.ShapeDtypeStruct((B, S, W), BF16),
        scratch_shapes=[pltpu.VMEM((RET_HEADS, RET_DK, RET_DV), F32),
                        pltpu.VMEM((RET_HEADS, tb, tb), F32)],
        compiler_params=_params(("arbitrary", "arbitrary")),
        name="retention",
    )(proj, proj, proj, proj, cosr, sinr)


def _oproj_router_kernel(ym_ref, yr_ref, wo_ref, x_ref, g1_ref, a2_ref, b2_ref, wrh_ref, wrl_ref, br_ref,
                         x1_ref, h2_ref, ri_ref, rw_ref, cnt_ref, carry_ref, *, tm):
    @pl.when(jnp.logical_and(pl.program_id(0) == 0, pl.program_id(1) == 0))
    def _():
        carry_ref[...] = jnp.zeros_like(carry_ref)

    acc = jnp.dot(ym_ref[...], wo_ref[:MIX_HALF, :], preferred_element_type=F32)
    acc = acc + jnp.dot(yr_ref[...], wo_ref[MIX_HALF:, :], preferred_element_type=F32)
    x1 = x_ref[...] + g1_ref[...] * acc
    x1_ref[...] = x1
    h2 = x1 * lax.rsqrt(jnp.mean(x1 * x1, axis=-1, keepdims=True) + EPS) * a2_ref[...] + b2_ref[...]
    h2_ref[...] = h2

    h_hi = h2.astype(BF16)
    h_lo = (h2 - h_hi.astype(F32)).astype(BF16)
    w_hi = wrh_ref[...]
    lg = (jnp.dot(h_hi, w_hi, preferred_element_type=F32) + jnp.dot(h_lo, w_hi, preferred_element_type=F32)
          + jnp.dot(h_hi, wrl_ref[...], preferred_element_type=F32) + br_ref[...])

    lane = lax.broadcasted_iota(I32, (tm, LANES), 1)
    big = jnp.int32(1 << 20)
    is_g = lane < N_GROUPS
    gl = jnp.where(is_g, lg, NEG_BIG)
    gmax = jnp.max(gl, axis=-1, keepdims=True)
    gsel = jnp.min(jnp.where(gl == gmax, lane, big), axis=-1, keepdims=True)
    p_group = 1.0 / jnp.sum(jnp.where(is_g, jnp.exp(gl - gmax), 0.0), axis=-1, keepdims=True)
    e_lane = lane - N_GROUPS
    in_grp = jnp.logical_and(jnp.logical_and(e_lane >= 0, e_lane < N_EXPERTS),
                             jnp.right_shift(e_lane, EPG_SHIFT) == gsel)
    el = jnp.where(in_grp, lg, NEG_BIG)
    t1 = jnp.max(el, axis=-1, keepdims=True)
    i1 = jnp.min(jnp.where(el == t1, lane, big), axis=-1, keepdims=True)
    el2 = jnp.where(lane == i1, NEG_BIG, el)
    t2 = jnp.max(el2, axis=-1, keepdims=True)
    i2 = jnp.min(jnp.where(el2 == t2, lane, big), axis=-1, keepdims=True)
    ex = jnp.exp(t2 - t1)
    w1 = p_group / (1.0 + ex)
    w2 = p_group * ex / (1.0 + ex)

    oh1 = lane == i1
    oh2 = lane == i2
    cnt = jnp.where(jnp.logical_or(oh1, oh2), 1.0, 0.0)
    row = lax.broadcasted_iota(I32, (tm, tm), 0)
    colm = lax.broadcasted_iota(I32, (tm, tm), 1)
    strict_lower = jnp.where(colm < row, 1.0, 0.0).astype(BF16)
    before = jnp.dot(strict_lower, cnt.astype(BF16), preferred_element_type=F32) + carry_ref[...]
    r1 = jnp.sum(jnp.where(oh1, before, 0.0), axis=-1, keepdims=True).astype(I32)
    r2 = jnp.sum(jnp.where(oh2, before, 0.0), axis=-1, keepdims=True).astype(I32)
    carry_ref[...] = carry_ref[...] + jnp.sum(cnt, axis=0, keepdims=True)
    cnt_ref[...] = carry_ref[...]

    zero_i = jnp.zeros((tm, LANES), I32)
    ri_ref[...] = jnp.where(lane == 0, i1 - N_GROUPS, jnp.where(lane == 1, i2 - N_GROUPS,
                            jnp.where(lane == 2, r1, jnp.where(lane == 3, r2, zero_i))))
    rw_ref[...] = jnp.where(lane == 0, w1, jnp.where(lane == 1, w2, 0.0))


def _oproj_router(y_mla, y_ret, wo, x, g1, a2, b2, wr_hi, wr_lo, br):
    B, S, D = x.shape
    tm = _tile(S, 256)
    row = lambda w: pl.BlockSpec((None, tm, w), lambda b, i: (b, i, 0))
    vec = pl.BlockSpec((None, 1, D), lambda b, i: (b, 0, 0))
    full = lambda a: pl.BlockSpec(a.shape, lambda b, i: (0,) * a.ndim)
    return pl.pallas_call(
        functools.partial(_oproj_router_kernel, tm=tm),
        grid=(B, S // tm),
        in_specs=[row(MIX_HALF), row(MIX_HALF), full(wo), row(D), vec, vec, vec,
                  full(wr_hi), full(wr_lo), full(br)],
        out_specs=[row(D), row(D), row(LANES), row(LANES), pl.BlockSpec((1, LANES), lambda b, i: (0, 0))],
        out_shape=[
            jax.ShapeDtypeStruct((B, S, D), F32),
            jax.ShapeDtypeStruct((B, S, D), F32),
            jax.ShapeDtypeStruct((B, S, LANES), I32),
            jax.ShapeDtypeStruct((B, S, LANES), F32),
            jax.ShapeDtypeStruct((1, LANES), F32),
        ],
        scratch_shapes=[pltpu.VMEM((1, LANES), F32)],
        compiler_params=_params(("arbitrary", "arbitrary")),
        name="oproj_router",
    )(y_mla, y_ret, wo, x, g1, a2, b2, wr_hi, wr_lo, br)


def _dispatch_kernel(pos_ref, h_ref, xs_in_ref, xs_ref, sem, *, tm, n_tok):
    del xs_in_ref
    base = pl.program_id(0) * tm

    def row_copy(r, p):
        return pltpu.make_async_copy(h_ref.at[pl.ds(r, 1)], xs_ref.at[pl.ds(p, 1)], sem)

    def issue(r, c):
        row_copy(r, pos_ref[base + r]).start()
        row_copy(r, pos_ref[n_tok + base + r]).start()
        return c

    def drain(r, c):
        row_copy(r, pos_ref[base + r]).wait()
        row_copy(r, pos_ref[n_tok + base + r]).wait()
        return c

    lax.fori_loop(0, tm, issue, 0)
    lax.fori_loop(0, tm, drain, 0)


def _dispatch(pos, h2, n_rows):
    T, D = h2.shape
    tm = _tile(T, 512)
    zeros = jnp.zeros((n_rows, D), F32)
    grid_spec = pltpu.PrefetchScalarGridSpec(
        num_scalar_prefetch=1,
        grid=(T // tm,),
        in_specs=[pl.BlockSpec((tm, D), lambda i, pos: (i, 0)),
                  pl.BlockSpec(memory_space=pl.ANY)],
        out_specs=pl.BlockSpec(memory_space=pl.ANY),
        scratch_shapes=[pltpu.SemaphoreType.DMA(())],
    )
    return pl.pallas_call(
        functools.partial(_dispatch_kernel, tm=tm, n_tok=T),
        grid_spec=grid_spec,
        out_shape=jax.ShapeDtypeStruct((n_rows, D), F32),
        input_output_aliases={2: 0},
        compiler_params=_params(("arbitrary",)),
        name="dispatch",
    )(pos, h2, zeros)


def _expert_kernel(te_ref, nt_ref, xs_ref, wg_ref, wu_ref, wd_ref, ys_ref):
    del te_ref
    used = pl.program_id(0) < nt_ref[0]

    @pl.when(jnp.logical_not(used))
    def _():
        ys_ref[...] = jnp.zeros_like(ys_ref)

    @pl.when(used)
    def _():
        xb = xs_ref[...].astype(BF16)
        a = jnp.dot(xb, wg_ref[...], preferred_element_type=F32)
        u = jnp.dot(xb, wu_ref[...], preferred_element_type=F32)
        hid = (a * jax.nn.sigmoid(a) * u).astype(BF16)
        ys_ref[...] = jnp.dot(hid, wd_ref[...], preferred_element_type=F32)


def _experts(tile_expert, n_tiles, xs, wg, wu, wd, te):
    P, D = xs.shape
    F = wg.shape[-1]
    row_idx = lambda i, te_ref, nt_ref: (jnp.minimum(i, nt_ref[0] - 1), 0)
    w_idx = lambda i, te_ref, nt_ref: (te_ref[i], 0, 0)
    grid_spec = pltpu.PrefetchScalarGridSpec(
        num_scalar_prefetch=2,
        grid=(P // te,),
        in_specs=[pl.BlockSpec((te, D), row_idx),
                  pl.BlockSpec((None, D, F), w_idx),
                  pl.BlockSpec((None, D, F), w_idx),
                  pl.BlockSpec((None, F, D), w_idx)],
        out_specs=pl.BlockSpec((te, D), lambda i, te_ref, nt_ref: (i, 0)),
    )
    return pl.pallas_call(
        _expert_kernel,
        grid_spec=grid_spec,
        out_shape=jax.ShapeDtypeStruct((P, D), F32),
        compiler_params=_params(("arbitrary",)),
        name="experts",
    )(tile_expert, n_tiles, xs, wg, wu, wd)


def _combine_kernel(pos_ref, x1_ref, rw_ref, g2_ref, fg_ref, ys_ref, o_ref, buf_ref, sem, *, tm, n_tok, seq, final):
    base = pl.program_id(0) * seq + pl.program_id(1) * tm

    def row_copy(k, r, p):
        return pltpu.make_async_copy(ys_ref.at[pl.ds(p, 1)], buf_ref.at[k, pl.ds(r, 1)], sem)

    def issue(r, c):
        row_copy(0, r, pos_ref[base + r]).start()
        row_copy(1, r, pos_ref[n_tok + base + r]).start()
        return c

    def drain(r, c):
        row_copy(0, r, pos_ref[base + r]).wait()
        row_copy(1, r, pos_ref[n_tok + base + r]).wait()
        return c

    lax.fori_loop(0, tm, issue, 0)
    lax.fori_loop(0, tm, drain, 0)
    w = rw_ref[...]
    moe = w[:, 0:1] * buf_ref[0] + w[:, 1:2] * buf_ref[1]
    x2 = x1_ref[...] + g2_ref[...] * moe
    if final:
        x2 = x2 * lax.rsqrt(jnp.mean(x2 * x2, axis=-1, keepdims=True) + EPS) * fg_ref[...]
    o_ref[...] = x2


def _combine(pos, x1, rw, g2, fg, ys, final):
    B, S, D = x1.shape
    tm = _tile(S, 512)
    grid_spec = pltpu.PrefetchScalarGridSpec(
        num_scalar_prefetch=1,
        grid=(B, S // tm),
        in_specs=[pl.BlockSpec((None, tm, D), lambda b, i, pos: (b, i, 0)),
                  pl.BlockSpec((None, tm, LANES), lambda b, i, pos: (b, i, 0)),
                  pl.BlockSpec((None, 1, D), lambda b, i, pos: (b, 0, 0)),
                  pl.BlockSpec((1, D), lambda b, i, pos: (0, 0)),
                  pl.BlockSpec(memory_space=pl.ANY)],
        out_specs=pl.BlockSpec((None, tm, D), lambda b, i, pos: (b, i, 0)),
        scratch_shapes=[pltpu.VMEM((2, tm, D), F32), pltpu.SemaphoreType.DMA(())],
    )
    return pl.pallas_call(
        functools.partial(_combine_kernel, tm=tm, n_tok=B * S, seq=S, final=final),
        grid_spec=grid_spec,
        out_shape=jax.ShapeDtypeStruct((B, S, D), F32),
        compiler_params=_params(("arbitrary", "arbitrary")),
        name="combine",
    )(pos, x1, rw, g2, fg, ys)


def _rope_tables(S, half):
    inv = ROPE_BASE ** (-jnp.arange(half, dtype=F32) / half)
    ang = jnp.arange(S, dtype=jnp.int32).astype(F32)[:, None] * inv[None, :]
    cos, sin = jnp.cos(ang), jnp.sin(ang)
    z = jnp.zeros((S, LANES // 2 - half), F32)
    return (jnp.concatenate([cos, z, cos, z], axis=-1), jnp.concatenate([-sin, z, sin, z], axis=-1))


def _layout_w_in(w_in):
    D = w_in.shape[0]
    o = np.cumsum([0, MLA_Q_RANK, MLA_KV_RANK, MLA_ROPE, 1024, 1024, 1024, 1024])
    half = MLA_ROPE // 2
    z = lambda n: jnp.zeros((D, n), w_in.dtype)
    kpe = w_in[:, o[2]:o[3]]
    cols = [w_in[:, o[0]:o[2]], kpe[:, :half], z(LANES // 2 - half), kpe[:, half:], z(LANES // 2 - half),
            z(PROJ_GROUP - MLA_Q_RANK - MLA_KV_RANK - LANES), w_in[:, o[3]:o[7]]]
    return jnp.concatenate(cols, axis=-1).astype(BF16)


def _layout_w_uq(w_uq):
    R = w_uq.shape[0]
    half = MLA_ROPE // 2
    w = w_uq.reshape(R, MLA_HEADS, MLA_NOPE + MLA_ROPE)
    z = jnp.zeros((R, MLA_HEADS, LANES // 2 - half), w_uq.dtype)
    w = jnp.concatenate([w[..., :MLA_NOPE], w[..., MLA_NOPE:MLA_NOPE + half], z, w[..., MLA_NOPE + half:], z], axis=-1)
    return w.reshape(R, MLA_HEADS * QK_WIDTH).astype(BF16)


def kernel(x, c, ada_w, ada_b, norm1_g, w_in, q_norm_g, w_uq, kv_norm_g, w_ukv, w_o, norm2_g, router_group_w, router_group_b, router_expert_w, router_expert_b, w_gate, w_up, w_down, final_norm_g):
    B, S, D = x.shape
    L = ada_w.shape[0]
    T = B * S
    te = 256
    n_rows = 2 * T + N_EXPERTS * te
    n_tiles_max = n_rows // te

    mod = _adaln_mod(c, ada_w, ada_b)
    cos_m, sin_m = _rope_tables(S, MLA_ROPE // 2)
    cos_r, sin_r = _rope_tables(S, RET_DK // 2)
    fg = final_norm_g.reshape(1, D)

    for l in range(L):
        sh1, sc1, g1, sh2, sc2, g2 = [mod[l, :, i * D:(i + 1) * D][:, None, :] for i in range(6)]
        a1 = norm1_g[l][None, None, :] * (1.0 + sc1)
        a2 = norm2_g[l][None, None, :] * (1.0 + sc2)

        proj = _norm_matmul(x, a1, sh1, _layout_w_in(w_in[l]))
        q, k, v = _mla_prep(proj, q_norm_g[l].reshape(1, -1), kv_norm_g[l].reshape(1, -1),
                            _layout_w_uq(w_uq[l]), w_ukv[l].astype(BF16), cos_m, sin_m)
        y_mla = _attention(q, k, v)
        y_ret = _retention(proj, cos_r, sin_r)

        wr = jnp.concatenate([router_group_w[l], router_expert_w[l],
                              jnp.zeros((D, LANES - N_GROUPS - N_EXPERTS), F32)], axis=-1)
        br = jnp.concatenate([router_group_b[l], router_expert_b[l],
                              jnp.zeros((LANES - N_GROUPS - N_EXPERTS,), F32)]).reshape(1, LANES)
        wr_hi = wr.astype(BF16)
        wr_lo = (wr - wr_hi.astype(F32)).astype(BF16)
        x1, h2, ri, rw, cnt = _oproj_router(y_mla, y_ret, w_o[l].astype(BF16), x, g1, a2, sh2, wr_hi, wr_lo, br)

        counts = cnt[0, N_GROUPS:N_GROUPS + N_EXPERTS].astype(I32)
        padded = ((counts + te - 1) // te) * te
        ends = jnp.cumsum(padded)
        off = ends - padded
        ri = ri.reshape(T, LANES)
        pos = jnp.concatenate([off[ri[:, 0]] + ri[:, 2], off[ri[:, 1]] + ri[:, 3]]).astype(I32)
        n_tiles = (ends[-1] // te).astype(I32)
        tile_ids = jnp.arange(n_tiles_max, dtype=I32)
        tile_expert = jnp.searchsorted(ends // te, jnp.minimum(tile_ids, n_tiles - 1), side="right").astype(I32)

        xs = _dispatch(pos, h2.reshape(T, D), n_rows)
        ys = _experts(tile_expert, n_tiles.reshape(1), xs, w_gate[l].astype(BF16), w_up[l].astype(BF16),
                      w_down[l].astype(BF16), te)
        x = _combine(pos, x1, rw, g2, fg, ys, final=(l == L - 1))
    return x
```

```python
import functools

import numpy as np
import jax
import jax.numpy as jnp
from jax import lax
from jax.experimental import pallas as pl
from jax.experimental.pallas import tpu as pltpu

F32 = jnp.float32
BF16 = jnp.bfloat16
I32 = jnp.int32
EPS = 1e-6
ROPE_BASE = 10000.0
CHUNK = 64

MLA_HEADS = 8
MLA_Q_RANK = 512
MLA_KV_RANK = 256
MLA_NOPE = 128
MLA_ROPE = 64
MLA_V = 128
RET_HEADS = 8
RET_DK = 128
RET_DV = 128
N_GROUPS = 4
EXPERTS_PER_GROUP = 8
N_EXPERTS = N_GROUPS * EXPERTS_PER_GROUP
CHUNK_SHIFT = CHUNK.bit_length() - 1
EPG_SHIFT = EXPERTS_PER_GROUP.bit_length() - 1

LANES = 128
QK_WIDTH = 2 * LANES
MIX_HALF = MLA_HEADS * MLA_V
PROJ_GROUP = 1024
PROJ_WIDTH = 5 * PROJ_GROUP
NEG_BIG = -1e30
VMEM_LIMIT = 56 * 1024 * 1024


def _tile(n, pref):
    if n <= pref:
        return n
    t = pref - pref % LANES
    while n % t:
        t -= LANES
    assert t > 0, (n, pref)
    return t


def _params(sem, vmem=VMEM_LIMIT):
    return pltpu.CompilerParams(dimension_semantics=sem, vmem_limit_bytes=vmem)


def _mod_kernel(c_ref, w_ref, b_ref, o_ref):
    c = c_ref[...]
    s = (c * jax.nn.sigmoid(c)).astype(BF16)
    o_ref[...] = jnp.dot(s, w_ref[...].astype(BF16), preferred_element_type=F32) + b_ref[...]


def _adaln_mod(c, ada_w, ada_b):
    B, D = c.shape
    L, _, N = ada_w.shape
    bp = 8
    cp = jnp.pad(c, ((0, bp - B), (0, 0)))
    tn = _tile(N, 1024)
    out = pl.pallas_call(
        _mod_kernel,
        grid=(L, N // tn),
        in_specs=[
            pl.BlockSpec((bp, D), lambda l, j: (0, 0)),
            pl.BlockSpec((None, D, tn), lambda l, j: (l, 0, j)),
            pl.BlockSpec((None, 1, tn), lambda l, j: (l, 0, j)),
        ],
        out_specs=pl.BlockSpec((None, bp, tn), lambda l, j: (l, 0, j)),
        out_shape=jax.ShapeDtypeStruct((L, bp, N), F32),
        compiler_params=_params(("arbitrary", "arbitrary")),
        name="adaln_mod",
    )(cp, ada_w, ada_b.reshape(L, 1, N))
    return out[:, :B]


def _norm_matmul_kernel(x_ref, a_ref, b_ref, w_ref, o_ref, h_ref):
    @pl.when(pl.program_id(2) == 0)
    def _():
        x = x_ref[...]
        ms = jnp.mean(x * x, axis=-1, keepdims=True)
        h_ref[...] = (x * lax.rsqrt(ms + EPS) * a_ref[...] + b_ref[...]).astype(BF16)

    o_ref[...] = jnp.dot(h_ref[...], w_ref[...], preferred_element_type=F32)


def _norm_matmul(x, a, b, w):
    B, S, D = x.shape
    N = w.shape[1]
    tm = _tile(S, 1024)
    tn = _tile(N, 640)
    return pl.pallas_call(
        _norm_matmul_kernel,
        grid=(B, S // tm, N // tn),
        in_specs=[
            pl.BlockSpec((None, tm, D), lambda b, i, j: (b, i, 0)),
            pl.BlockSpec((None, 1, D), lambda b, i, j: (b, 0, 0)),
            pl.BlockSpec((None, 1, D), lambda b, i, j: (b, 0, 0)),
            pl.BlockSpec((D, tn), lambda b, i, j: (0, j)),
        ],
        out_specs=pl.BlockSpec((None, tm, tn), lambda b, i, j: (b, i, j)),
        out_shape=jax.ShapeDtypeStruct((B, S, N), F32),
        scratch_shapes=[pltpu.VMEM((tm, D), BF16)],
        compiler_params=_params(("arbitrary", "arbitrary", "arbitrary")),
        name="norm_matmul",
    )(x, a, b, w)


def _rms(x, g):
    return x * lax.rsqrt(jnp.mean(x * x, axis=-1, keepdims=True) + EPS) * g


def _rope_block(x, cosb, sinb):
    return x * cosb + pltpu.roll(x, LANES // 2, 1) * sinb


def _mla_prep_kernel(p_ref, qg_ref, kvg_ref, wuq_ref, wukv_ref, cos_ref, sin_ref, q_ref, k_ref, v_ref):
    p = p_ref[...]
    cq = p[:, :MLA_Q_RANK]
    ckv = p[:, MLA_Q_RANK:MLA_Q_RANK + MLA_KV_RANK]
    kpe = p[:, MLA_Q_RANK + MLA_KV_RANK:MLA_Q_RANK + MLA_KV_RANK + LANES]
    cosb = cos_ref[...]
    sinb = sin_ref[...]
    scale = (MLA_NOPE + MLA_ROPE) ** -0.5 * float(np.log2(np.e))
    qf = jnp.dot(_rms(cq, qg_ref[...]).astype(BF16), wuq_ref[...], preferred_element_type=F32)
    kvf = jnp.dot(_rms(ckv, kvg_ref[...]).astype(BF16), wukv_ref[...], preferred_element_type=F32)
    kpe_r = _rope_block(kpe, cosb, sinb).astype(BF16)
    for h in range(MLA_HEADS):
        c0 = h * QK_WIDTH
        q_ref[h, :, :LANES] = (qf[:, c0:c0 + LANES] * scale).astype(BF16)
        q_ref[h, :, LANES:] = (_rope_block(qf[:, c0 + LANES:c0 + QK_WIDTH], cosb, sinb) * scale).astype(BF16)
        k_ref[h, :, :LANES] = kvf[:, c0:c0 + LANES].astype(BF16)
        k_ref[h, :, LANES:] = kpe_r
        v_ref[h] = kvf[:, c0 + LANES:c0 + QK_WIDTH].astype(BF16)


def _mla_prep(proj, qg, kvg, wuq, wukv, cosb, sinb):
    B, S, _ = proj.shape
    H = MLA_HEADS
    tm = _tile(S, 512)
    return pl.pallas_call(
        _mla_prep_kernel,
        grid=(B, S // tm),
        in_specs=[
            pl.BlockSpec((None, tm, PROJ_GROUP), lambda b, i: (b, i, 0)),
            pl.BlockSpec((1, MLA_Q_RANK), lambda b, i: (0, 0)),
            pl.BlockSpec((1, MLA_KV_RANK), lambda b, i: (0, 0)),
            pl.BlockSpec((MLA_Q_RANK, H * QK_WIDTH), lambda b, i: (0, 0)),
            pl.BlockSpec((MLA_KV_RANK, H * QK_WIDTH), lambda b, i: (0, 0)),
            pl.BlockSpec((tm, LANES), lambda b, i: (i, 0)),
            pl.BlockSpec((tm, LANES), lambda b, i: (i, 0)),
        ],
        out_specs=[
            pl.BlockSpec((None, H, tm, QK_WIDTH), lambda b, i: (b, 0, i, 0)),
            pl.BlockSpec((None, H, tm, QK_WIDTH), lambda b, i: (b, 0, i, 0)),
            pl.BlockSpec((None, H, tm, MLA_V), lambda b, i: (b, 0, i, 0)),
        ],
        out_shape=[
            jax.ShapeDtypeStruct((B, H, S, QK_WIDTH), BF16),
            jax.ShapeDtypeStruct((B, H, S, QK_WIDTH), BF16),
            jax.ShapeDtypeStruct((B, H, S, MLA_V), BF16),
        ],
        compiler_params=_params(("arbitrary", "arbitrary")),
        name="mla_prep",
    )(proj, qg, kvg, wuq, wukv, cosb, sinb)


_NT = (((1,), (1,)), ((), ()))
_TN = (((0,), (0,)), ((), ()))


def _attn_kernel(q_ref, k_ref, v_ref, o_ref, *, tq, tk, hp):
    i = pl.program_id(2)
    sub = tq // tk

    def scores(h, j):
        start = pl.multiple_of(j * tk, tk)
        return lax.dot_general(q_ref[h], k_ref[h, pl.ds(start, tk), :], _NT, preferred_element_type=F32)

    def update(h, j, s, m, l, acc, masked_sub):
        if masked_sub is not None:
            qc = jnp.right_shift(lax.broadcasted_iota(I32, (tq, tk), 0), CHUNK_SHIFT)
            kc = jnp.right_shift(lax.broadcasted_iota(I32, (tq, tk), 1) + masked_sub * tk, CHUNK_SHIFT)
            s = jnp.where(kc <= qc, s, NEG_BIG)
        m_new = jnp.maximum(m, jnp.max(s, axis=-1, keepdims=True))
        alpha = jnp.exp2(m - m_new)
        p = jnp.exp2(s - m_new)
        l = alpha * l + jnp.sum(p, axis=-1, keepdims=True)
        start = pl.multiple_of(j * tk, tk)
        acc = alpha * acc + jnp.dot(p.astype(BF16), v_ref[h, pl.ds(start, tk), :], preferred_element_type=F32)
        return m_new, l, acc

    def body(j, carry):
        return tuple(update(h, j, scores(h, j), *carry[h], None) for h in range(hp))

    init = tuple((jnp.full((tq, 1), NEG_BIG, F32), jnp.zeros((tq, 1), F32), jnp.zeros((tq, MLA_V), F32))
                 for h in range(hp))
    n_full = i * sub
    carry = lax.fori_loop(0, n_full, body, init)
    for h in range(hp):
        m, l, acc = carry[h]
        for jj in range(sub):
            m, l, acc = update(h, n_full + jj, scores(h, n_full + jj), m, l, acc, jj)
        o_ref[:, h * MLA_V:(h + 1) * MLA_V] = (acc / l).astype(o_ref.dtype)


def _attention(q, k, v):
    B, H, S, _ = q.shape
    tq = _tile(S, 512)
    tk = _tile(tq, 512)
    hp = 4
    return pl.pallas_call(
        functools.partial(_attn_kernel, tq=tq, tk=tk, hp=hp),
        grid=(B, H // hp, S // tq),
        in_specs=[
            pl.BlockSpec((None, hp, tq, QK_WIDTH), lambda b, h, i: (b, h, i, 0)),
            pl.BlockSpec((None, hp, S, QK_WIDTH), lambda b, h, i: (b, h, 0, 0)),
            pl.BlockSpec((None, hp, S, MLA_V), lambda b, h, i: (b, h, 0, 0)),
        ],
        out_specs=pl.BlockSpec((None, tq, hp * MLA_V), lambda b, h, i: (b, i, h)),
        out_shape=jax.ShapeDtypeStruct((B, S, H * MLA_V), BF16),
        compiler_params=_params(("arbitrary", "arbitrary", "arbitrary")),
        name="attention",
    )(q, k, v)


def _retention_kernel(q_ref, k_ref, v_ref, g_ref, cos_ref, sin_ref, o_ref, state_ref, decay_ref, *, tb):
    first_block = pl.program_id(1) == 0
    log_g = [float(np.log1p(-np.exp2(-5.0 - h))) for h in range(RET_HEADS)]

    @pl.when(jnp.logical_and(pl.program_id(0) == 0, first_block))
    def _():
        rel = (lax.broadcasted_iota(I32, (tb, tb), 0) - lax.broadcasted_iota(I32, (tb, tb), 1)).astype(F32)
        for h in range(RET_HEADS):
            decay_ref[h] = jnp.where(rel >= 0, jnp.exp(jnp.maximum(rel, 0.0) * log_g[h]), 0.0)

    @pl.when(first_block)
    def _():
        state_ref[...] = jnp.zeros_like(state_ref)

    cosr = cos_ref[...]
    sinr = sin_ref[...]
    idx = lax.broadcasted_iota(I32, (tb, 1), 0).astype(F32)
    for h in range(RET_HEADS):
        sl = slice(h * RET_DK, (h + 1) * RET_DK)
        q = _rope_block(q_ref[:, sl], cosr, sinr)
        k = _rope_block(k_ref[:, sl], cosr, sinr) * (RET_DK ** -0.5)
        vb = v_ref[:, sl].astype(BF16)
        scores = lax.dot_general(q.astype(BF16), k.astype(BF16), _NT, preferred_element_type=F32) * decay_ref[h]
        y = jnp.dot(scores.astype(BF16), vb, preferred_element_type=F32)
        state = state_ref[h]
        q_dec = q * jnp.exp((idx + 1.0) * log_g[h])
        y = y + jnp.dot(q_dec.astype(BF16), state.astype(BF16), preferred_element_type=F32)
        k_dec = k * jnp.exp((tb - 1.0 - idx) * log_g[h])
        upd = lax.dot_general(k_dec.astype(BF16), vb, _TN, preferred_element_type=F32)
        state_ref[h] = state * float(np.exp(tb * log_g[h])) + upd
        mu = jnp.mean(y, axis=-1, keepdims=True)
        yc = y - mu
        var = jnp.mean(yc * yc, axis=-1, keepdims=True)
        g = g_ref[:, sl]
        o_ref[:, sl] = (g * jax.nn.sigmoid(g) * (yc * lax.rsqrt(var + EPS))).astype(o_ref.dtype)


def _retention(proj, cosr, sinr):
    B, S, _ = proj.shape
    tb = _tile(S, 256)
    W = RET_HEADS * RET_DK
    col = lambda g: pl.BlockSpec((None, tb, W), lambda b, i: (b, i, g))
    return pl.pallas_call(
        functools.partial(_retention_kernel, tb=tb),
        grid=(B, S // tb),
        in_specs=[col(1), col(2), col(3), col(4),
                  pl.BlockSpec((tb, LANES), lambda b, i: (i, 0)),
                  pl.BlockSpec((tb, LANES), lambda b, i: (i, 0))],
        out_specs=pl.BlockSpec((None, tb, W), lambda b, i: (b, i, 0)),
        out_shape=jax.ShapeDtypeStruct((B, S, W), BF16),
        scratch_shapes=[pltpu.VMEM((RET_HEADS, RET_DK, RET_DV), F32),
                        pltpu.VMEM((RET_HEADS, tb, tb), F32)],
        compiler_params=_params(("arbitrary", "arbitrary")),
        name="retention",
    )(proj, proj, proj, proj, cosr, sinr)


def _oproj_router_kernel(ym_ref, yr_ref, wo_ref, x_ref, g1_ref, a2_ref, b2_ref, wrh_ref, wrl_ref, br_ref,
                         x1_ref, h2_ref, ri_ref, rw_ref, cnt_ref, carry_ref, *, tm):
    @pl.when(jnp.logical_and(pl.program_id(0) == 0, pl.program_id(1) == 0))
    def _():
        carry_ref[...] = jnp.zeros_like(carry_ref)

    acc = jnp.dot(ym_ref[...], wo_ref[:MIX_HALF, :], preferred_element_type=F32)
    acc = acc + jnp.dot(yr_ref[...], wo_ref[MIX_HALF:, :], preferred_element_type=F32)
    x1 = x_ref[...] + g1_ref[...] * acc
    x1_ref[...] = x1
    h2 = x1 * lax.rsqrt(jnp.mean(x1 * x1, axis=-1, keepdims=True) + EPS) * a2_ref[...] + b2_ref[...]
    h2_ref[...] = h2

    h_hi = h2.astype(BF16)
    h_lo = (h2 - h_hi.astype(F32)).astype(BF16)
    w_hi = wrh_ref[...]
    lg = (jnp.dot(h_hi, w_hi, preferred_element_type=F32) + jnp.dot(h_lo, w_hi, preferred_element_type=F32)
          + jnp.dot(h_hi, wrl_ref[...], preferred_element_type=F32) + br_ref[...])

    lane = lax.broadcasted_iota(I32, (tm, LANES), 1)
    big = jnp.int32(1 << 20)
    is_g = lane < N_GROUPS
    gl = jnp.where(is_g, lg, NEG_BIG)
    gmax = jnp.max(gl, axis=-1, keepdims=True)
    gsel = jnp.min(jnp.where(gl == gmax, lane, big), axis=-1, keepdims=True)
    p_group = 1.0 / jnp.sum(jnp.where(is_g, jnp.exp(gl - gmax), 0.0), axis=-1, keepdims=True)
    e_lane = lane - N_GROUPS
    in_grp = jnp.logical_and(jnp.logical_and(e_lane >= 0, e_lane < N_EXPERTS),
                             jnp.right_shift(e_lane, EPG_SHIFT) == gsel)
    el = jnp.where(in_grp, lg, NEG_BIG)
    t1 = jnp.max(el, axis=-1, keepdims=True)
    i1 = jnp.min(jnp.where(el == t1, lane, big), axis=-1, keepdims=True)
    el2 = jnp.where(lane == i1, NEG_BIG, el)
    t2 = jnp.max(el2, axis=-1, keepdims=True)
    i2 = jnp.min(jnp.where(el2 == t2, lane, big), axis=-1, keepdims=True)
    ex = jnp.exp(t2 - t1)
    w1 = p_group / (1.0 + ex)
    w2 = p_group * ex / (1.0 + ex)

    oh1 = lane == i1
    oh2 = lane == i2
    cnt = jnp.where(jnp.logical_or(oh1, oh2), 1.0, 0.0)
    row = lax.broadcasted_iota(I32, (tm, tm), 0)
    colm = lax.broadcasted_iota(I32, (tm, tm), 1)
    strict_lower = jnp.where(colm < row, 1.0, 0.0).astype(BF16)
    before = jnp.dot(strict_lower, cnt.astype(BF16), preferred_element_type=F32) + carry_ref[...]
    r1 = jnp.sum(jnp.where(oh1, before, 0.0), axis=-1, keepdims=True).astype(I32)
    r2 = jnp.sum(jnp.where(oh2, before, 0.0), axis=-1, keepdims=True).astype(I32)
    carry_ref[...] = carry_ref[...] + jnp.sum(cnt, axis=0, keepdims=True)
    cnt_ref[...] = carry_ref[...]

    zero_i = jnp.zeros((tm, LANES), I32)
    ri_ref[...] = jnp.where(lane == 0, i1 - N_GROUPS, jnp.where(lane == 1, i2 - N_GROUPS,
                            jnp.where(lane == 2, r1, jnp.where(lane == 3, r2, zero_i))))
    rw_ref[...] = jnp.where(lane == 0, w1, jnp.where(lane == 1, w2, 0.0))


def _oproj_router(y_mla, y_ret, wo, x, g1, a2, b2, wr_hi, wr_lo, br):
    B, S, D = x.shape
    tm = _tile(S, 256)
    row = lambda w: pl.BlockSpec((None, tm, w), lambda b, i: (b, i, 0))
    vec = pl.BlockSpec((None, 1, D), lambda b, i: (b, 0, 0))
    full = lambda a: pl.BlockSpec(a.shape, lambda b, i: (0,) * a.ndim)
    return pl.pallas_call(
        functools.partial(_oproj_router_kernel, tm=tm),
        grid=(B, S // tm),
        in_specs=[row(MIX_HALF), row(MIX_HALF), full(wo), row(D), vec, vec, vec,
                  full(wr_hi), full(wr_lo), full(br)],
        out_specs=[row(D), row(D), row(LANES), row(LANES), pl.BlockSpec((1, LANES), lambda b, i: (0, 0))],
        out_shape=[
            jax.ShapeDtypeStruct((B, S, D), F32),
            jax.ShapeDtypeStruct((B, S, D), F32),
            jax.ShapeDtypeStruct((B, S, LANES), I32),
            jax.ShapeDtypeStruct((B, S, LANES), F32),
            jax.ShapeDtypeStruct((1, LANES), F32),
        ],
        scratch_shapes=[pltpu.VMEM((1, LANES), F32)],
        compiler_params=_params(("arbitrary", "arbitrary")),
        name="oproj_router",
    )(y_mla, y_ret, wo, x, g1, a2, b2, wr_hi, wr_lo, br)


def _dispatch_kernel(pos_ref, zt_ref, h_ref, xs_ref, zero_ref, sem, zsem, *, tm, n_tok, te):
    base = pl.program_id(0) * tm

    @pl.when(pl.program_id(0) == 0)
    def _():
        zero_ref[...] = jnp.zeros_like(zero_ref)

        def zero_copy(n):
            row = pl.multiple_of(zt_ref[n] * te, te)
            return pltpu.make_async_copy(zero_ref, xs_ref.at[pl.ds(row, te)], zsem)

        for n in range(zt_ref.shape[0]):
            @pl.when(zt_ref[n] >= 0)
            def _():
                zero_copy(n).start()
        for n in range(zt_ref.shape[0]):
            @pl.when(zt_ref[n] >= 0)
            def _():
                zero_copy(n).wait()

    def row_copy(r, p):
        return pltpu.make_async_copy(h_ref.at[pl.ds(r, 1)], xs_ref.at[pl.ds(p, 1)], sem)

    def issue(r, c):
        row_copy(r, pos_ref[base + r]).start()
        row_copy(r, pos_ref[n_tok + base + r]).start()
        return c

    lax.fori_loop(0, tm, issue, 0, unroll=8)
    for _ in range(2):
        pltpu.make_async_copy(h_ref, xs_ref.at[pl.ds(0, tm)], sem).wait()


def _dispatch(pos, zero_tiles, h2, n_rows, te):
    T, D = h2.shape
    tm = _tile(T, 512)
    grid_spec = pltpu.PrefetchScalarGridSpec(
        num_scalar_prefetch=2,
        grid=(T // tm,),
        in_specs=[pl.BlockSpec((tm, D), lambda i, pos, zt: (i, 0))],
        out_specs=pl.BlockSpec(memory_space=pl.ANY),
        scratch_shapes=[pltpu.VMEM((te, D), F32), pltpu.SemaphoreType.DMA(()), pltpu.SemaphoreType.DMA(())],
    )
    return pl.pallas_call(
        functools.partial(_dispatch_kernel, tm=tm, n_tok=T, te=te),
        grid_spec=grid_spec,
        out_shape=jax.ShapeDtypeStruct((n_rows, D), F32),
        compiler_params=_params(("arbitrary",)),
        name="dispatch",
    )(pos, zero_tiles, h2)


def _expert_kernel(te_ref, nt_ref, xs_ref, wg_ref, wu_ref, wd_ref, ys_ref, wgb_ref, wub_ref, wdb_ref):
    i = pl.program_id(0)
    used = i < nt_ref[0]
    new_expert = jnp.logical_or(i == 0, te_ref[i] != te_ref[jnp.maximum(i - 1, 0)])

    @pl.when(jnp.logical_and(used, new_expert))
    def _():
        wgb_ref[...] = wg_ref[...].astype(BF16)
        wub_ref[...] = wu_ref[...].astype(BF16)
        wdb_ref[...] = wd_ref[...].astype(BF16)

    @pl.when(jnp.logical_not(used))
    def _():
        ys_ref[...] = jnp.zeros_like(ys_ref)

    @pl.when(used)
    def _():
        xb = xs_ref[...].astype(BF16)
        a = jnp.dot(xb, wgb_ref[...], preferred_element_type=F32)
        u = jnp.dot(xb, wub_ref[...], preferred_element_type=F32)
        hid = (a * jax.nn.sigmoid(a) * u).astype(BF16)
        ys_ref[...] = jnp.dot(hid, wdb_ref[...], preferred_element_type=F32)


def _experts(tile_expert, n_tiles, xs, wg, wu, wd, te):
    P, D = xs.shape
    F = wg.shape[-1]
    row_idx = lambda i, te_ref, nt_ref: (jnp.minimum(i, nt_ref[0] - 1), 0)
    w_idx = lambda i, te_ref, nt_ref: (te_ref[i], 0, 0)
    grid_spec = pltpu.PrefetchScalarGridSpec(
        num_scalar_prefetch=2,
        grid=(P // te,),
        in_specs=[pl.BlockSpec((te, D), row_idx),
                  pl.BlockSpec((None, D, F), w_idx),
                  pl.BlockSpec((None, D, F), w_idx),
                  pl.BlockSpec((None, F, D), w_idx)],
        out_specs=pl.BlockSpec((te, D), lambda i, te_ref, nt_ref: (i, 0)),
        scratch_shapes=[pltpu.VMEM((D, F), BF16), pltpu.VMEM((D, F), BF16), pltpu.VMEM((F, D), BF16)],
    )
    return pl.pallas_call(
        _expert_kernel,
        grid_spec=grid_spec,
        out_shape=jax.ShapeDtypeStruct((P, D), F32),
        compiler_params=_params(("arbitrary",)),
        name="experts",
    )(tile_expert, n_tiles, xs, wg, wu, wd)


def _combine_kernel(pos_ref, x1_ref, rw_ref, g2_ref, fg_ref, ys_ref, o_ref, buf_ref, sem, *, tm, n_tok, seq, final):
    base = pl.program_id(0) * seq + pl.program_id(1) * tm

    def row_copy(k, r, p):
        return pltpu.make_async_copy(ys_ref.at[pl.ds(p, 1)], buf_ref.at[k, pl.ds(r, 1)], sem)

    def issue(r, c):
        row_copy(0, r, pos_ref[base + r]).start()
        row_copy(1, r, pos_ref[n_tok + base + r]).start()
        return c

    lax.fori_loop(0, tm, issue, 0, unroll=8)
    for k in range(2):
        pltpu.make_async_copy(ys_ref.at[pl.ds(0, tm)], buf_ref.at[k], sem).wait()
    w = rw_ref[...]
    moe = w[:, 0:1] * buf_ref[0] + w[:, 1:2] * buf_ref[1]
    x2 = x1_ref[...] + g2_ref[...] * moe
    if final:
        x2 = x2 * lax.rsqrt(jnp.mean(x2 * x2, axis=-1, keepdims=True) + EPS) * fg_ref[...]
    o_ref[...] = x2


def _combine(pos, x1, rw, g2, fg, ys, final):
    B, S, D = x1.shape
    tm = _tile(S, 512)
    grid_spec = pltpu.PrefetchScalarGridSpec(
        num_scalar_prefetch=1,
        grid=(B, S // tm),
        in_specs=[pl.BlockSpec((None, tm, D), lambda b, i, pos: (b, i, 0)),
                  pl.BlockSpec((None, tm, LANES), lambda b, i, pos: (b, i, 0)),
                  pl.BlockSpec((None, 1, D), lambda b, i, pos: (b, 0, 0)),
                  pl.BlockSpec((1, D), lambda b, i, pos: (0, 0)),
                  pl.BlockSpec(memory_space=pl.ANY)],
        out_specs=pl.BlockSpec((None, tm, D), lambda b, i, pos: (b, i, 0)),
        scratch_shapes=[pltpu.VMEM((2, tm, D), F32), pltpu.SemaphoreType.DMA(())],
    )
    return pl.pallas_call(
        functools.partial(_combine_kernel, tm=tm, n_tok=B * S, seq=S, final=final),
        grid_spec=grid_spec,
        out_shape=jax.ShapeDtypeStruct((B, S, D), F32),
        compiler_params=_params(("arbitrary", "arbitrary")),
        name="combine",
    )(pos, x1, rw, g2, fg, ys)


def _rope_tables(S, half):
    inv = ROPE_BASE ** (-jnp.arange(half, dtype=F32) / half)
    ang = jnp.arange(S, dtype=jnp.int32).astype(F32)[:, None] * inv[None, :]
    cos, sin = jnp.cos(ang), jnp.sin(ang)
    z = jnp.zeros((S, LANES // 2 - half), F32)
    return (jnp.concatenate([cos, z, cos, z], axis=-1), jnp.concatenate([-sin, z, sin, z], axis=-1))


def _layout_w_in(w_in):
    D = w_in.shape[0]
    o = np.cumsum([0, MLA_Q_RANK, MLA_KV_RANK, MLA_ROPE, 1024, 1024, 1024, 1024])
    half = MLA_ROPE // 2
    z = lambda n: jnp.zeros((D, n), w_in.dtype)
    kpe = w_in[:, o[2]:o[3]]
    cols = [w_in[:, o[0]:o[2]], kpe[:, :half], z(LANES // 2 - half), kpe[:, half:], z(LANES // 2 - half),
            z(PROJ_GROUP - MLA_Q_RANK - MLA_KV_RANK - LANES), w_in[:, o[3]:o[7]]]
    return jnp.concatenate(cols, axis=-1).astype(BF16)


def _layout_w_uq(w_uq):
    R = w_uq.shape[0]
    half = MLA_ROPE // 2
    w = w_uq.reshape(R, MLA_HEADS, MLA_NOPE + MLA_ROPE)
    z = jnp.zeros((R, MLA_HEADS, LANES // 2 - half), w_uq.dtype)
    w = jnp.concatenate([w[..., :MLA_NOPE], w[..., MLA_NOPE:MLA_NOPE + half], z, w[..., MLA_NOPE + half:], z], axis=-1)
    return w.reshape(R, MLA_HEADS * QK_WIDTH).astype(BF16)


def kernel(x, c, ada_w, ada_b, norm1_g, w_in, q_norm_g, w_uq, kv_norm_g, w_ukv, w_o, norm2_g, router_group_w, router_group_b, router_expert_w, router_expert_b, w_gate, w_up, w_down, final_norm_g):
    B, S, D = x.shape
    L = ada_w.shape[0]
    T = B * S
    te = 256
    n_rows = 2 * T + N_EXPERTS * te
    n_tiles_max = n_rows // te

    mod = _adaln_mod(c, ada_w, ada_b)
    cos_m, sin_m = _rope_tables(S, MLA_ROPE // 2)
    cos_r, sin_r = _rope_tables(S, RET_DK // 2)
    fg = final_norm_g.reshape(1, D)

    for l in range(L):
        sh1, sc1, g1, sh2, sc2, g2 = [mod[l, :, i * D:(i + 1) * D][:, None, :] for i in range(6)]
        a1 = norm1_g[l][None, None, :] * (1.0 + sc1)
        a2 = norm2_g[l][None, None, :] * (1.0 + sc2)

        proj = _norm_matmul(x, a1, sh1, _layout_w_in(w_in[l]))
        q, k, v = _mla_prep(proj, q_norm_g[l].reshape(1, -1), kv_norm_g[l].reshape(1, -1),
                            _layout_w_uq(w_uq[l]), w_ukv[l].astype(BF16), cos_m, sin_m)
        y_mla = _attention(q, k, v)
        y_ret = _retention(proj, cos_r, sin_r)

        wr = jnp.concatenate([router_group_w[l], router_expert_w[l],
                              jnp.zeros((D, LANES - N_GROUPS - N_EXPERTS), F32)], axis=-1)
        br = jnp.concatenate([router_group_b[l], router_expert_b[l],
                              jnp.zeros((LANES - N_GROUPS - N_EXPERTS,), F32)]).reshape(1, LANES)
        wr_hi = wr.astype(BF16)
        wr_lo = (wr - wr_hi.astype(F32)).astype(BF16)
        x1, h2, ri, rw, cnt = _oproj_router(y_mla, y_ret, w_o[l].astype(BF16), x, g1, a2, sh2, wr_hi, wr_lo, br)

        counts = cnt[0, N_GROUPS:N_GROUPS + N_EXPERTS].astype(I32)
        padded = ((counts + te - 1) // te) * te
        ends = jnp.cumsum(padded)
        off = ends - padded
        ri = ri.reshape(T, LANES)
        pos = jnp.concatenate([off[ri[:, 0]] + ri[:, 2], off[ri[:, 1]] + ri[:, 3]]).astype(I32)
        n_tiles = (ends[-1] // te).astype(I32)
        tile_ids = jnp.minimum(jnp.arange(n_tiles_max, dtype=I32), n_tiles - 1)
        tile_expert = jnp.sum((ends // te)[None, :] <= tile_ids[:, None], axis=1).astype(I32)
        spare = n_tiles + jnp.arange(n_tiles_max - 2 * T // te, dtype=I32)
        zero_tiles = jnp.concatenate([jnp.where(padded > counts, ends // te - 1, -1),
                                      jnp.where(spare < n_tiles_max, spare, -1)]).astype(I32)

        xs = _dispatch(pos, zero_tiles, h2.reshape(T, D), n_rows, te)
        ys = _experts(tile_expert, n_tiles.reshape(1), xs, w_gate[l], w_up[l], w_down[l], te)
        x = _combine(pos, x1, rw, g2, fg, ys, final=(l == L - 1))
    return x
```

```python
import functools

import numpy as np
import jax
import jax.numpy as jnp
from jax import lax
from jax.experimental import pallas as pl
from jax.experimental.pallas import tpu as pltpu

F32 = jnp.float32
BF16 = jnp.bfloat16
I32 = jnp.int32
EPS = 1e-6
ROPE_BASE = 10000.0
CHUNK = 64

MLA_HEADS = 8
MLA_Q_RANK = 512
MLA_KV_RANK = 256
MLA_NOPE = 128
MLA_ROPE = 64
MLA_V = 128
RET_HEADS = 8
RET_DK = 128
RET_DV = 128
N_GROUPS = 4
EXPERTS_PER_GROUP = 8
N_EXPERTS = N_GROUPS * EXPERTS_PER_GROUP
CHUNK_SHIFT = CHUNK.bit_length() - 1
EPG_SHIFT = EXPERTS_PER_GROUP.bit_length() - 1

LANES = 128
QK_WIDTH = 2 * LANES
MIX_HALF = MLA_HEADS * MLA_V
PROJ_GROUP = 1024
PROJ_WIDTH = 5 * PROJ_GROUP
NEG_BIG = -1e30
VMEM_LIMIT = 56 * 1024 * 1024


def _tile(n, pref):
    if n <= pref:
        return n
    t = pref - pref % LANES
    while n % t:
        t -= LANES
    assert t > 0, (n, pref)
    return t


def _params(sem, vmem=VMEM_LIMIT):
    return pltpu.CompilerParams(dimension_semantics=sem, vmem_limit_bytes=vmem)


def _mod_kernel(c_ref, w_ref, b_ref, o_ref):
    c = c_ref[...]
    s = (c * jax.nn.sigmoid(c)).astype(BF16)
    o_ref[...] = jnp.dot(s, w_ref[...].astype(BF16), preferred_element_type=F32) + b_ref[...]


def _adaln_mod(c, ada_w, ada_b):
    B, D = c.shape
    L, _, N = ada_w.shape
    bp = 8
    cp = jnp.pad(c, ((0, bp - B), (0, 0)))
    tn = _tile(N, 1024)
    out = pl.pallas_call(
        _mod_kernel,
        grid=(L, N // tn),
        in_specs=[
            pl.BlockSpec((bp, D), lambda l, j: (0, 0)),
            pl.BlockSpec((None, D, tn), lambda l, j: (l, 0, j)),
            pl.BlockSpec((None, 1, tn), lambda l, j: (l, 0, j)),
        ],
        out_specs=pl.BlockSpec((None, bp, tn), lambda l, j: (l, 0, j)),
        out_shape=jax.ShapeDtypeStruct((L, bp, N), F32),
        compiler_params=_params(("arbitrary", "arbitrary")),
        name="adaln_mod",
    )(cp, ada_w, ada_b.reshape(L, 1, N))
    return out[:, :B]


def _norm_matmul_kernel(x_ref, a_ref, b_ref, w_ref, o_ref, h_ref):
    @pl.when(pl.program_id(2) == 0)
    def _():
        x = x_ref[...]
        ms = jnp.mean(x * x, axis=-1, keepdims=True)
        h_ref[...] = (x * lax.rsqrt(ms + EPS) * a_ref[...] + b_ref[...]).astype(BF16)

    o_ref[...] = jnp.dot(h_ref[...], w_ref[...], preferred_element_type=F32).astype(o_ref.dtype)


def _norm_matmul(x, a, b, w):
    B, S, D = x.shape
    N = w.shape[1]
    tm = _tile(S, 1024)
    tn = _tile(N, 1024)
    return pl.pallas_call(
        _norm_matmul_kernel,
        grid=(B, S // tm, N // tn),
        in_specs=[
            pl.BlockSpec((None, tm, D), lambda b, i, j: (b, i, 0)),
            pl.BlockSpec((None, 1, D), lambda b, i, j: (b, 0, 0)),
            pl.BlockSpec((None, 1, D), lambda b, i, j: (b, 0, 0)),
            pl.BlockSpec((D, tn), lambda b, i, j: (0, j)),
        ],
        out_specs=pl.BlockSpec((None, tm, tn), lambda b, i, j: (b, i, j)),
        out_shape=jax.ShapeDtypeStruct((B, S, N), BF16),
        scratch_shapes=[pltpu.VMEM((tm, D), BF16)],
        compiler_params=_params(("arbitrary", "arbitrary", "arbitrary")),
        name="norm_matmul",
    )(x, a, b, w)


def _rms(x, g):
    return x * lax.rsqrt(jnp.mean(x * x, axis=-1, keepdims=True) + EPS) * g


def _rope_block(x, cosb, sinb):
    return x * cosb + pltpu.roll(x, LANES // 2, 1) * sinb


def _mla_prep_kernel(p_ref, qg_ref, kvg_ref, wuq_ref, wukv_ref, cos_ref, sin_ref, q_ref, k_ref, v_ref):
    p = p_ref[...].astype(F32)
    cq = p[:, :MLA_Q_RANK]
    ckv = p[:, MLA_Q_RANK:MLA_Q_RANK + MLA_KV_RANK]
    kpe = p[:, MLA_Q_RANK + MLA_KV_RANK:MLA_Q_RANK + MLA_KV_RANK + LANES]
    cosb = cos_ref[...]
    sinb = sin_ref[...]
    scale = (MLA_NOPE + MLA_ROPE) ** -0.5 * float(np.log2(np.e))
    qf = jnp.dot(_rms(cq, qg_ref[...]).astype(BF16), wuq_ref[...], preferred_element_type=F32)
    kvf = jnp.dot(_rms(ckv, kvg_ref[...]).astype(BF16), wukv_ref[...], preferred_element_type=F32)
    kpe_r = _rope_block(kpe, cosb, sinb).astype(BF16)
    for h in range(MLA_HEADS):
        c0 = h * QK_WIDTH
        q_ref[h, :, :LANES] = (qf[:, c0:c0 + LANES] * scale).astype(BF16)
        q_ref[h, :, LANES:] = (_rope_block(qf[:, c0 + LANES:c0 + QK_WIDTH], cosb, sinb) * scale).astype(BF16)
        k_ref[h, :, :LANES] = kvf[:, c0:c0 + LANES].astype(BF16)
        k_ref[h, :, LANES:] = kpe_r
        v_ref[h] = kvf[:, c0 + LANES:c0 + QK_WIDTH].astype(BF16)


def _mla_prep(proj, qg, kvg, wuq, wukv, cosb, sinb):
    B, S, _ = proj.shape
    H = MLA_HEADS
    tm = _tile(S, 512)
    return pl.pallas_call(
        _mla_prep_kernel,
        grid=(B, S // tm),
        in_specs=[
            pl.BlockSpec((None, tm, PROJ_GROUP), lambda b, i: (b, i, 0)),
            pl.BlockSpec((1, MLA_Q_RANK), lambda b, i: (0, 0)),
            pl.BlockSpec((1, MLA_KV_RANK), lambda b, i: (0, 0)),
            pl.BlockSpec((MLA_Q_RANK, H * QK_WIDTH), lambda b, i: (0, 0)),
            pl.BlockSpec((MLA_KV_RANK, H * QK_WIDTH), lambda b, i: (0, 0)),
            pl.BlockSpec((tm, LANES), lambda b, i: (i, 0)),
            pl.BlockSpec((tm, LANES), lambda b, i: (i, 0)),
        ],
        out_specs=[
            pl.BlockSpec((None, H, tm, QK_WIDTH), lambda b, i: (b, 0, i, 0)),
            pl.BlockSpec((None, H, tm, QK_WIDTH), lambda b, i: (b, 0, i, 0)),
            pl.BlockSpec((None, H, tm, MLA_V), lambda b, i: (b, 0, i, 0)),
        ],
        out_shape=[
            jax.ShapeDtypeStruct((B, H, S, QK_WIDTH), BF16),
            jax.ShapeDtypeStruct((B, H, S, QK_WIDTH), BF16),
            jax.ShapeDtypeStruct((B, H, S, MLA_V), BF16),
        ],
        compiler_params=_params(("arbitrary", "arbitrary")),
        name="mla_prep",
    )(proj, qg, kvg, wuq, wukv, cosb, sinb)


_NT = (((1,), (1,)), ((), ()))
_TN = (((0,), (0,)), ((), ()))


def _attn_kernel(q_ref, k_ref, v_ref, o_ref, *, tq, tk, hp):
    i = pl.program_id(2)
    sub = tq // tk

    def scores(h, j):
        start = pl.multiple_of(j * tk, tk)
        return lax.dot_general(q_ref[h], k_ref[h, pl.ds(start, tk), :], _NT, preferred_element_type=F32)

    def update(h, j, s, m, l, acc, masked_sub):
        if masked_sub is not None:
            qc = jnp.right_shift(lax.broadcasted_iota(I32, (tq, tk), 0), CHUNK_SHIFT)
            kc = jnp.right_shift(lax.broadcasted_iota(I32, (tq, tk), 1) + masked_sub * tk, CHUNK_SHIFT)
            s = jnp.where(kc <= qc, s, NEG_BIG)
        m_new = jnp.maximum(m, jnp.max(s, axis=-1, keepdims=True))
        alpha = jnp.exp2(m - m_new)
        p = jnp.exp2(s - m_new)
        l = alpha * l + jnp.sum(p, axis=-1, keepdims=True)
        start = pl.multiple_of(j * tk, tk)
        acc = alpha * acc + jnp.dot(p.astype(BF16), v_ref[h, pl.ds(start, tk), :], preferred_element_type=F32)
        return m_new, l, acc

    def body(j, carry):
        return tuple(update(h, j, scores(h, j), *carry[h], None) for h in range(hp))

    init = tuple((jnp.full((tq, 1), NEG_BIG, F32), jnp.zeros((tq, 1), F32), jnp.zeros((tq, MLA_V), F32))
                 for h in range(hp))
    n_full = i * sub
    carry = lax.fori_loop(0, n_full, body, init)
    for h in range(hp):
        m, l, acc = carry[h]
        for jj in range(sub):
            m, l, acc = update(h, n_full + jj, scores(h, n_full + jj), m, l, acc, jj)
        o_ref[:, h * MLA_V:(h + 1) * MLA_V] = (acc / l).astype(o_ref.dtype)


def _attention(q, k, v):
    B, H, S, _ = q.shape
    tq = _tile(S, 512)
    tk = _tile(tq, 512)
    hp = 4
    return pl.pallas_call(
        functools.partial(_attn_kernel, tq=tq, tk=tk, hp=hp),
        grid=(B, H // hp, S // tq),
        in_specs=[
            pl.BlockSpec((None, hp, tq, QK_WIDTH), lambda b, h, i: (b, h, i, 0)),
            pl.BlockSpec((None, hp, S, QK_WIDTH), lambda b, h, i: (b, h, 0, 0)),
            pl.BlockSpec((None, hp, S, MLA_V), lambda b, h, i: (b, h, 0, 0)),
        ],
        out_specs=pl.BlockSpec((None, tq, hp * MLA_V), lambda b, h, i: (b, i, h)),
        out_shape=jax.ShapeDtypeStruct((B, S, H * MLA_V), BF16),
        compiler_params=_params(("arbitrary", "arbitrary", "arbitrary")),
        name="attention",
    )(q, k, v)


def _retention_kernel(q_ref, k_ref, v_ref, g_ref, cos_ref, sin_ref, o_ref,
                      state_ref, decay_ref, qdec_ref, kdec_ref, *, tb):
    first_block = pl.program_id(1) == 0
    log_g = [float(np.log1p(-np.exp2(-5.0 - h))) for h in range(RET_HEADS)]

    @pl.when(jnp.logical_and(pl.program_id(0) == 0, first_block))
    def _():
        rel = (lax.broadcasted_iota(I32, (tb, tb), 0) - lax.broadcasted_iota(I32, (tb, tb), 1)).astype(F32)
        idx = lax.broadcasted_iota(I32, (tb, RET_DK), 0).astype(F32)
        for h in range(RET_HEADS):
            decay_ref[h] = jnp.where(rel >= 0, jnp.exp(jnp.maximum(rel, 0.0) * log_g[h]), 0.0)
            qdec_ref[h] = jnp.exp((idx + 1.0) * log_g[h])
            kdec_ref[h] = jnp.exp((tb - 1.0 - idx) * log_g[h])

    @pl.when(first_block)
    def _():
        state_ref[...] = jnp.zeros_like(state_ref)

    cosr = cos_ref[...]
    sinr = sin_ref[...]
    for h in range(RET_HEADS):
        sl = slice(h * RET_DK, (h + 1) * RET_DK)
        q = _rope_block(q_ref[:, sl].astype(F32), cosr, sinr)
        k = _rope_block(k_ref[:, sl].astype(F32), cosr, sinr) * (RET_DK ** -0.5)
        vb = v_ref[:, sl]
        scores = lax.dot_general(q.astype(BF16), k.astype(BF16), _NT, preferred_element_type=F32) * decay_ref[h]
        y = jnp.dot(scores.astype(BF16), vb, preferred_element_type=F32)
        state = state_ref[h]
        q_dec = q * qdec_ref[h]
        y = y + jnp.dot(q_dec.astype(BF16), state.astype(BF16), preferred_element_type=F32)
        k_dec = k * kdec_ref[h]
        upd = lax.dot_general(k_dec.astype(BF16), vb, _TN, preferred_element_type=F32)
        state_ref[h] = state * float(np.exp(tb * log_g[h])) + upd
        mu = jnp.mean(y, axis=-1, keepdims=True)
        yc = y - mu
        var = jnp.mean(yc * yc, axis=-1, keepdims=True)
        g = g_ref[:, sl].astype(F32)
        o_ref[:, sl] = (g * jax.nn.sigmoid(g) * (yc * lax.rsqrt(var + EPS))).astype(o_ref.dtype)


def _retention(proj, cosr, sinr):
    B, S, _ = proj.shape
    tb = _tile(S, 256)
    W = RET_HEADS * RET_DK
    col = lambda g: pl.BlockSpec((None, tb, W), lambda b, i: (b, i, g))
    return pl.pallas_call(
        functools.partial(_retention_kernel, tb=tb),
        grid=(B, S // tb),
        in_specs=[col(1), col(2), col(3), col(4),
                  pl.BlockSpec((tb, LANES), lambda b, i: (i, 0)),
                  pl.BlockSpec((tb, LANES), lambda b, i: (i, 0))],
        out_specs=pl.BlockSpec((None, tb, W), lambda b, i: (b, i, 0)),
        out_shape=jax.ShapeDtypeStruct((B, S, W), BF16),
        scratch_shapes=[pltpu.VMEM((RET_HEADS, RET_DK, RET_DV), F32),
                        pltpu.VMEM((RET_HEADS, tb, tb), F32),
                        pltpu.VMEM((RET_HEADS, tb, RET_DK), F32),
                        pltpu.VMEM((RET_HEADS, tb, RET_DK), F32)],
        compiler_params=_params(("arbitrary", "arbitrary")),
        name="retention",
    )(proj, proj, proj, proj, cosr, sinr)


def _oproj_router_kernel(ym_ref, yr_ref, wo_ref, x_ref, g1_ref, a2_ref, b2_ref, wr_ref, br_ref,
                         x1_ref, h2_ref, ri_ref, rw_ref, cnt_ref, carry_ref, *, tm, ts):
    @pl.when(jnp.logical_and(pl.program_id(0) == 0, pl.program_id(1) == 0))
    def _():
        carry_ref[...] = jnp.zeros_like(carry_ref)

    lane = lax.broadcasted_iota(I32, (ts, LANES), 1)
    big = jnp.int32(1 << 20)
    is_g = lane < N_GROUPS
    e_lane = lane - N_GROUPS
    is_e = jnp.logical_and(e_lane >= 0, e_lane < N_EXPERTS)
    strict_lower = jnp.where(lax.broadcasted_iota(I32, (ts, ts), 1) < lax.broadcasted_iota(I32, (ts, ts), 0),
                             1.0, 0.0).astype(BF16)
    carry = carry_ref[...]

    for r0 in range(0, tm, ts):
        rows = pl.ds(r0, ts)
        acc = jnp.dot(ym_ref[rows, :], wo_ref[:MIX_HALF, :], preferred_element_type=F32)
        acc = acc + jnp.dot(yr_ref[rows, :], wo_ref[MIX_HALF:, :], preferred_element_type=F32)
        x1 = x_ref[rows, :] + g1_ref[...] * acc
        x1_ref[rows, :] = x1
        h2 = x1 * lax.rsqrt(jnp.mean(x1 * x1, axis=-1, keepdims=True) + EPS) * a2_ref[...] + b2_ref[...]
        h2_ref[rows, :] = h2

        h_hi = h2.astype(BF16)
        h_lo = (h2 - h_hi.astype(F32)).astype(BF16)
        lg2 = (jnp.dot(h_hi, wr_ref[...], preferred_element_type=F32)
               + jnp.dot(h_lo, wr_ref[...], preferred_element_type=F32))
        lg = lg2[:, :LANES] + lg2[:, LANES:] + br_ref[...]

        gl = jnp.where(is_g, lg, NEG_BIG)
        gmax = jnp.max(gl, axis=-1, keepdims=True)
        gsel = jnp.min(jnp.where(gl == gmax, lane, big), axis=-1, keepdims=True)
        p_group = 1.0 / jnp.sum(jnp.where(is_g, jnp.exp(gl - gmax), 0.0), axis=-1, keepdims=True)
        in_grp = jnp.logical_and(is_e, jnp.right_shift(e_lane, EPG_SHIFT) == gsel)
        el = jnp.where(in_grp, lg, NEG_BIG)
        t1 = jnp.max(el, axis=-1, keepdims=True)
        i1 = jnp.min(jnp.where(el == t1, lane, big), axis=-1, keepdims=True)
        el2 = jnp.where(lane == i1, NEG_BIG, el)
        t2 = jnp.max(el2, axis=-1, keepdims=True)
        i2 = jnp.min(jnp.where(el2 == t2, lane, big), axis=-1, keepdims=True)
        ex = jnp.exp(t2 - t1)
        w1 = p_group / (1.0 + ex)
        w2 = p_group * ex / (1.0 + ex)

        oh1 = lane == i1
        oh2 = lane == i2
        cnt = jnp.where(jnp.logical_or(oh1, oh2), 1.0, 0.0)
        before = jnp.dot(strict_lower, cnt.astype(BF16), preferred_element_type=F32) + carry
        r1 = jnp.sum(jnp.where(oh1, before, 0.0), axis=-1, keepdims=True).astype(I32)
        r2 = jnp.sum(jnp.where(oh2, before, 0.0), axis=-1, keepdims=True).astype(I32)
        carry = carry + jnp.sum(cnt, axis=0, keepdims=True)

        zero_i = jnp.zeros((ts, LANES), I32)
        ri_ref[rows, :] = jnp.where(lane == 0, i1 - N_GROUPS, jnp.where(lane == 1, i2 - N_GROUPS,
                                    jnp.where(lane == 2, r1, jnp.where(lane == 3, r2, zero_i))))
        rw_ref[rows, :] = jnp.where(lane == 0, w1, jnp.where(lane == 1, w2, 0.0))

    carry_ref[...] = carry
    cnt_ref[...] = carry


def _oproj_router(y_mla, y_ret, wo, x, g1, a2, b2, wr, br):
    B, S, D = x.shape
    tm = _tile(S, 512)
    row = lambda w: pl.BlockSpec((None, tm, w), lambda b, i: (b, i, 0))
    vec = pl.BlockSpec((None, 1, D), lambda b, i: (b, 0, 0))
    full = lambda a: pl.BlockSpec(a.shape, lambda b, i: (0,) * a.ndim, pipeline_mode=pl.Buffered(1))
    return pl.pallas_call(
        functools.partial(_oproj_router_kernel, tm=tm, ts=tm),
        grid=(B, S // tm),
        in_specs=[row(MIX_HALF), row(MIX_HALF), full(wo), row(D), vec, vec, vec, full(wr), full(br)],
        out_specs=[row(D), row(D), row(LANES), row(LANES), pl.BlockSpec((1, LANES), lambda b, i: (0, 0))],
        out_shape=[
            jax.ShapeDtypeStruct((B, S, D), F32),
            jax.ShapeDtypeStruct((B, S, D), F32),
            jax.ShapeDtypeStruct((B, S, LANES), I32),
            jax.ShapeDtypeStruct((B, S, LANES), F32),
            jax.ShapeDtypeStruct((1, LANES), F32),
        ],
        scratch_shapes=[pltpu.VMEM((1, LANES), F32)],
        compiler_params=_params(("arbitrary", "arbitrary")),
        name="oproj_router",
    )(y_mla, y_ret, wo, x, g1, a2, b2, wr, br)


def _dispatch_kernel(pos_ref, zt_ref, h_ref, xs_ref, zero_ref, sem, zsem, *, tm, n_tok, te):
    base = pl.program_id(0) * tm

    @pl.when(pl.program_id(0) == 0)
    def _():
        zero_ref[...] = jnp.zeros_like(zero_ref)

        def zero_copy(n):
            row = pl.multiple_of(zt_ref[n] * te, te)
            return pltpu.make_async_copy(zero_ref, xs_ref.at[pl.ds(row, te)], zsem)

        for n in range(zt_ref.shape[0]):
            @pl.when(zt_ref[n] >= 0)
            def _():
                zero_copy(n).start()
        for n in range(zt_ref.shape[0]):
            @pl.when(zt_ref[n] >= 0)
            def _():
                zero_copy(n).wait()

    def row_copy(r, p):
        return pltpu.make_async_copy(h_ref.at[pl.ds(r, 1)], xs_ref.at[pl.ds(p, 1)], sem)

    def issue(r, c):
        row_copy(r, pos_ref[base + r]).start()
        row_copy(r, pos_ref[n_tok + base + r]).start()
        return c

    lax.fori_loop(0, tm, issue, 0, unroll=8)
    for _ in range(2):
        pltpu.make_async_copy(h_ref, xs_ref.at[pl.ds(0, tm)], sem).wait()


def _dispatch(pos, zero_tiles, h2, n_rows, te):
    T, D = h2.shape
    tm = _tile(T, 512)
    grid_spec = pltpu.PrefetchScalarGridSpec(
        num_scalar_prefetch=2,
        grid=(T // tm,),
        in_specs=[pl.BlockSpec((tm, D), lambda i, pos, zt: (i, 0))],
        out_specs=pl.BlockSpec(memory_space=pl.ANY),
        scratch_shapes=[pltpu.VMEM((te, D), F32), pltpu.SemaphoreType.DMA(()), pltpu.SemaphoreType.DMA(())],
    )
    return pl.pallas_call(
        functools.partial(_dispatch_kernel, tm=tm, n_tok=T, te=te),
        grid_spec=grid_spec,
        out_shape=jax.ShapeDtypeStruct((n_rows, D), F32),
        compiler_params=_params(("arbitrary",)),
        name="dispatch",
    )(pos, zero_tiles, h2)


def _expert_kernel(te_ref, nx_ref, nt_ref, xs_ref, wg_hbm, wu_hbm, wd_hbm, ys_ref,
                   wgf_ref, wuf_ref, wdf_ref, wgb_ref, wub_ref, wdb_ref, sem, *, layer):
    i = pl.program_id(0)
    used = i < nt_ref[0]
    expert = te_ref[i]
    new_expert = jnp.logical_or(i == 0, expert != te_ref[jnp.maximum(i - 1, 0)])

    def fetch(e):
        return (pltpu.make_async_copy(wg_hbm.at[layer, e], wgf_ref, sem.at[0]),
                pltpu.make_async_copy(wu_hbm.at[layer, e], wuf_ref, sem.at[1]),
                pltpu.make_async_copy(wd_hbm.at[layer, e], wdf_ref, sem.at[2]))

    @pl.when(i == 0)
    def _():
        for c in fetch(expert):
            c.start()

    @pl.when(jnp.logical_and(used, new_expert))
    def _():
        for c in fetch(expert):
            c.wait()
        wgb_ref[...] = wgf_ref[...].astype(BF16)
        wub_ref[...] = wuf_ref[...].astype(BF16)
        wdb_ref[...] = wdf_ref[...].astype(BF16)

        @pl.when(nx_ref[i] >= 0)
        def _():
            for c in fetch(nx_ref[i]):
                c.start()

    @pl.when(jnp.logical_not(used))
    def _():
        ys_ref[...] = jnp.zeros_like(ys_ref)

    @pl.when(used)
    def _():
        xb = xs_ref[...].astype(BF16)
        a = jnp.dot(xb, wgb_ref[...], preferred_element_type=F32)
        u = jnp.dot(xb, wub_ref[...], preferred_element_type=F32)
        hid = (a * jax.nn.sigmoid(a) * u).astype(BF16)
        ys_ref[...] = jnp.dot(hid, wdb_ref[...], preferred_element_type=F32)


def _experts(tile_expert, next_expert, n_tiles, xs, wg, wu, wd, layer, te):
    P, D = xs.shape
    F = wg.shape[-1]
    row_idx = lambda i, te_ref, nx_ref, nt_ref: (jnp.minimum(i, nt_ref[0] - 1), 0)
    hbm = pl.BlockSpec(memory_space=pl.ANY)
    grid_spec = pltpu.PrefetchScalarGridSpec(
        num_scalar_prefetch=3,
        grid=(P // te,),
        in_specs=[pl.BlockSpec((te, D), row_idx), hbm, hbm, hbm],
        out_specs=pl.BlockSpec((te, D), lambda i, te_ref, nx_ref, nt_ref: (i, 0)),
        scratch_shapes=[pltpu.VMEM((D, F), F32), pltpu.VMEM((D, F), F32), pltpu.VMEM((F, D), F32),
                        pltpu.VMEM((D, F), BF16), pltpu.VMEM((D, F), BF16), pltpu.VMEM((F, D), BF16),
                        pltpu.SemaphoreType.DMA((3,))],
    )
    return pl.pallas_call(
        functools.partial(_expert_kernel, layer=layer),
        grid_spec=grid_spec,
        out_shape=jax.ShapeDtypeStruct((P, D), F32),
        compiler_params=_params(("arbitrary",)),
        name="experts",
    )(tile_expert, next_expert, n_tiles, xs, wg, wu, wd)


def _combine_kernel(pos_ref, x1_ref, rw_ref, g2_ref, fg_ref, ys_ref, o_ref, buf_ref, sem, *, tm, n_tok, seq, final):
    base = pl.program_id(0) * seq + pl.program_id(1) * tm

    def row_copy(k, r, p):
        return pltpu.make_async_copy(ys_ref.at[pl.ds(p, 1)], buf_ref.at[k, pl.ds(r, 1)], sem)

    def issue(r, c):
        row_copy(0, r, pos_ref[base + r]).start()
        row_copy(1, r, pos_ref[n_tok + base + r]).start()
        return c

    lax.fori_loop(0, tm, issue, 0, unroll=8)
    for k in range(2):
        pltpu.make_async_copy(ys_ref.at[pl.ds(0, tm)], buf_ref.at[k], sem).wait()
    w = rw_ref[...]
    moe = w[:, 0:1] * buf_ref[0] + w[:, 1:2] * buf_ref[1]
    x2 = x1_ref[...] + g2_ref[...] * moe
    if final:
        x2 = x2 * lax.rsqrt(jnp.mean(x2 * x2, axis=-1, keepdims=True) + EPS) * fg_ref[...]
    o_ref[...] = x2


def _combine(pos, x1, rw, g2, fg, ys, final):
    B, S, D = x1.shape
    tm = _tile(S, 512)
    grid_spec = pltpu.PrefetchScalarGridSpec(
        num_scalar_prefetch=1,
        grid=(B, S // tm),
        in_specs=[pl.BlockSpec((None, tm, D), lambda b, i, pos: (b, i, 0)),
                  pl.BlockSpec((None, tm, LANES), lambda b, i, pos: (b, i, 0)),
                  pl.BlockSpec((None, 1, D), lambda b, i, pos: (b, 0, 0)),
                  pl.BlockSpec((1, D), lambda b, i, pos: (0, 0)),
                  pl.BlockSpec(memory_space=pl.ANY)],
        out_specs=pl.BlockSpec((None, tm, D), lambda b, i, pos: (b, i, 0)),
        scratch_shapes=[pltpu.VMEM((2, tm, D), F32), pltpu.SemaphoreType.DMA(())],
    )
    return pl.pallas_call(
        functools.partial(_combine_kernel, tm=tm, n_tok=B * S, seq=S, final=final),
        grid_spec=grid_spec,
        out_shape=jax.ShapeDtypeStruct((B, S, D), F32),
        compiler_params=_params(("arbitrary", "arbitrary")),
        name="combine",
    )(pos, x1, rw, g2, fg, ys)


def _rope_tables(S, half):
    inv = ROPE_BASE ** (-jnp.arange(half, dtype=F32) / half)
    ang = jnp.arange(S, dtype=jnp.int32).astype(F32)[:, None] * inv[None, :]
    cos, sin = jnp.cos(ang), jnp.sin(ang)
    z = jnp.zeros((S, LANES // 2 - half), F32)
    return (jnp.concatenate([cos, z, cos, z], axis=-1), jnp.concatenate([-sin, z, sin, z], axis=-1))


def _layout_w_in(w_in):
    D = w_in.shape[0]
    o = np.cumsum([0, MLA_Q_RANK, MLA_KV_RANK, MLA_ROPE, 1024, 1024, 1024, 1024])
    half = MLA_ROPE // 2
    z = lambda n: jnp.zeros((D, n), w_in.dtype)
    kpe = w_in[:, o[2]:o[3]]
    cols = [w_in[:, o[0]:o[2]], kpe[:, :half], z(LANES // 2 - half), kpe[:, half:], z(LANES // 2 - half),
            z(PROJ_GROUP - MLA_Q_RANK - MLA_KV_RANK - LANES), w_in[:, o[3]:o[7]]]
    return jnp.concatenate(cols, axis=-1).astype(BF16)


def _layout_w_uq(w_uq):
    R = w_uq.shape[0]
    half = MLA_ROPE // 2
    w = w_uq.reshape(R, MLA_HEADS, MLA_NOPE + MLA_ROPE)
    z = jnp.zeros((R, MLA_HEADS, LANES // 2 - half), w_uq.dtype)
    w = jnp.concatenate([w[..., :MLA_NOPE], w[..., MLA_NOPE:MLA_NOPE + half], z, w[..., MLA_NOPE + half:], z], axis=-1)
    return w.reshape(R, MLA_HEADS * QK_WIDTH).astype(BF16)


def kernel(x, c, ada_w, ada_b, norm1_g, w_in, q_norm_g, w_uq, kv_norm_g, w_ukv, w_o, norm2_g, router_group_w, router_group_b, router_expert_w, router_expert_b, w_gate, w_up, w_down, final_norm_g):
    B, S, D = x.shape
    L = ada_w.shape[0]
    T = B * S
    te = 256
    n_rows = 2 * T + N_EXPERTS * te
    n_tiles_max = n_rows // te

    mod = _adaln_mod(c, ada_w, ada_b)
    cos_m, sin_m = _rope_tables(S, MLA_ROPE // 2)
    cos_r, sin_r = _rope_tables(S, RET_DK // 2)
    fg = final_norm_g.reshape(1, D)

    for l in range(L):
        sh1, sc1, g1, sh2, sc2, g2 = [mod[l, :, i * D:(i + 1) * D][:, None, :] for i in range(6)]
        a1 = norm1_g[l][None, None, :] * (1.0 + sc1)
        a2 = norm2_g[l][None, None, :] * (1.0 + sc2)

        proj = _norm_matmul(x, a1, sh1, _layout_w_in(w_in[l]))
        q, k, v = _mla_prep(proj, q_norm_g[l].reshape(1, -1), kv_norm_g[l].reshape(1, -1),
                            _layout_w_uq(w_uq[l]), w_ukv[l].astype(BF16), cos_m, sin_m)
        y_mla = _attention(q, k, v)
        y_ret = _retention(proj, cos_r, sin_r)

        wr = jnp.concatenate([router_group_w[l], router_expert_w[l],
                              jnp.zeros((D, LANES - N_GROUPS - N_EXPERTS), F32)], axis=-1)
        br = jnp.concatenate([router_group_b[l], router_expert_b[l],
                              jnp.zeros((LANES - N_GROUPS - N_EXPERTS,), F32)]).reshape(1, LANES)
        wr_hi = wr.astype(BF16)
        wr_lo = (wr - wr_hi.astype(F32)).astype(BF16)
        x1, h2, ri, rw, cnt = _oproj_router(y_mla, y_ret, w_o[l].astype(BF16), x, g1, a2, sh2,
                                            jnp.concatenate([wr_hi, wr_lo], axis=-1), br)

        counts = cnt[0, N_GROUPS:N_GROUPS + N_EXPERTS].astype(I32)
        padded = ((counts + te - 1) // te) * te
        ends = jnp.cumsum(padded)
        off = ends - padded
        ri = ri.reshape(T, LANES)
        pos = jnp.concatenate([off[ri[:, 0]] + ri[:, 2], off[ri[:, 1]] + ri[:, 3]]).astype(I32)
        n_tiles = (ends[-1] // te).astype(I32)
        tile_ids = jnp.minimum(jnp.arange(n_tiles_max, dtype=I32), n_tiles - 1)
        tile_expert = jnp.sum((ends // te)[None, :] <= tile_ids[:, None], axis=1).astype(I32)
        spare = n_tiles + jnp.arange(n_tiles_max - 2 * T // te, dtype=I32)
        zero_tiles = jnp.concatenate([jnp.where(padded > counts, ends // te - 1, -1),
                                      jnp.where(spare < n_tiles_max, spare, -1)]).astype(I32)

        xs = _dispatch(pos, zero_tiles, h2.reshape(T, D), n_rows, te)
        eids = jnp.arange(N_EXPERTS, dtype=I32)
        later = jnp.logical_and(eids[None, :] > eids[:, None], (padded > 0)[None, :])
        next_of = jnp.min(jnp.where(later, eids[None, :], N_EXPERTS), axis=1)
        next_of = jnp.where(next_of < N_EXPERTS, next_of, -1)
        next_expert = jnp.sum(jnp.where(tile_expert[:, None] == eids[None, :], next_of[None, :], 0), axis=1).astype(I32)
        ys = _experts(tile_expert, next_expert, n_tiles.reshape(1), xs, w_gate, w_up, w_down, l, te)
        x = _combine(pos, x1, rw, g2, fg, ys, final=(l == L - 1))
    return x
```

```python
import functools

import numpy as np
import jax
import jax.numpy as jnp
from jax import lax
from jax.experimental import pallas as pl
from jax.experimental.pallas import tpu as pltpu

F32 = jnp.float32
BF16 = jnp.bfloat16
I32 = jnp.int32
EPS = 1e-6
ROPE_BASE = 10000.0
CHUNK = 64

MLA_HEADS = 8
MLA_Q_RANK = 512
MLA_KV_RANK = 256
MLA_NOPE = 128
MLA_ROPE = 64
MLA_V = 128
RET_HEADS = 8
RET_DK = 128
RET_DV = 128
N_GROUPS = 4
EXPERTS_PER_GROUP = 8
N_EXPERTS = N_GROUPS * EXPERTS_PER_GROUP
CHUNK_SHIFT = CHUNK.bit_length() - 1
EPG_SHIFT = EXPERTS_PER_GROUP.bit_length() - 1

LANES = 128
QK_WIDTH = 2 * LANES
MIX_HALF = MLA_HEADS * MLA_V
PROJ_GROUP = 1024
PROJ_WIDTH = 5 * PROJ_GROUP
NEG_BIG = -1e30
VMEM_LIMIT = 56 * 1024 * 1024


def _tile(n, pref):
    if n <= pref:
        return n
    t = pref - pref % LANES
    while n % t:
        t -= LANES
    assert t > 0, (n, pref)
    return t


def _params(sem, vmem=VMEM_LIMIT):
    return pltpu.CompilerParams(dimension_semantics=sem, vmem_limit_bytes=vmem)


def _mod_kernel(c_ref, w_ref, b_ref, o_ref):
    c = c_ref[...]
    s = (c * jax.nn.sigmoid(c)).astype(BF16)
    o_ref[...] = jnp.dot(s, w_ref[...].astype(BF16), preferred_element_type=F32) + b_ref[...]


def _adaln_mod(c, ada_w, ada_b):
    B, D = c.shape
    L, _, N = ada_w.shape
    bp = 8
    cp = jnp.pad(c, ((0, bp - B), (0, 0)))
    tn = _tile(N, 1024)
    out = pl.pallas_call(
        _mod_kernel,
        grid=(L, N // tn),
        in_specs=[
            pl.BlockSpec((bp, D), lambda l, j: (0, 0)),
            pl.BlockSpec((None, D, tn), lambda l, j: (l, 0, j)),
            pl.BlockSpec((None, 1, tn), lambda l, j: (l, 0, j)),
        ],
        out_specs=pl.BlockSpec((None, bp, tn), lambda l, j: (l, 0, j)),
        out_shape=jax.ShapeDtypeStruct((L, bp, N), F32),
        compiler_params=_params(("arbitrary", "arbitrary")),
        name="adaln_mod",
    )(cp, ada_w, ada_b.reshape(L, 1, N))
    return out[:, :B]


def _norm_matmul_kernel(x_ref, a_ref, b_ref, w_ref, o_ref, h_ref):
    @pl.when(pl.program_id(2) == 0)
    def _():
        x = x_ref[...]
        ms = jnp.mean(x * x, axis=-1, keepdims=True)
        h_ref[...] = (x * lax.rsqrt(ms + EPS) * a_ref[...] + b_ref[...]).astype(BF16)

    o_ref[...] = jnp.dot(h_ref[...], w_ref[...], preferred_element_type=F32).astype(o_ref.dtype)


def _norm_matmul(x, a, b, w):
    B, S, D = x.shape
    N = w.shape[1]
    tm = _tile(S, 1024)
    tn = _tile(N, 1024)
    return pl.pallas_call(
        _norm_matmul_kernel,
        grid=(B, S // tm, N // tn),
        in_specs=[
            pl.BlockSpec((None, tm, D), lambda b, i, j: (b, i, 0)),
            pl.BlockSpec((None, 1, D), lambda b, i, j: (b, 0, 0)),
            pl.BlockSpec((None, 1, D), lambda b, i, j: (b, 0, 0)),
            pl.BlockSpec((D, tn), lambda b, i, j: (0, j)),
        ],
        out_specs=pl.BlockSpec((None, tm, tn), lambda b, i, j: (b, i, j)),
        out_shape=jax.ShapeDtypeStruct((B, S, N), BF16),
        scratch_shapes=[pltpu.VMEM((tm, D), BF16)],
        compiler_params=_params(("arbitrary", "arbitrary", "arbitrary")),
        name="norm_matmul",
    )(x, a, b, w)


def _rms(x, g):
    return x * lax.rsqrt(jnp.mean(x * x, axis=-1, keepdims=True) + EPS) * g


def _rope_block(x, cosb, sinb):
    return x * cosb + pltpu.roll(x, LANES // 2, 1) * sinb


def _mla_prep_kernel(p_ref, qg_ref, kvg_ref, wuq_ref, wukv_ref, cos_ref, sin_ref, q_ref, k_ref, v_ref):
    p = p_ref[...].astype(F32)
    cq = p[:, :MLA_Q_RANK]
    ckv = p[:, MLA_Q_RANK:MLA_Q_RANK + MLA_KV_RANK]
    kpe = p[:, MLA_Q_RANK + MLA_KV_RANK:MLA_Q_RANK + MLA_KV_RANK + LANES]
    cosb = cos_ref[...]
    sinb = sin_ref[...]
    scale = (MLA_NOPE + MLA_ROPE) ** -0.5 * float(np.log2(np.e))
    qf = jnp.dot(_rms(cq, qg_ref[...]).astype(BF16), wuq_ref[...], preferred_element_type=F32)
    kvf = jnp.dot(_rms(ckv, kvg_ref[...]).astype(BF16), wukv_ref[...], preferred_element_type=F32)
    kpe_r = _rope_block(kpe, cosb, sinb).astype(BF16)
    for h in range(MLA_HEADS):
        c0 = h * QK_WIDTH
        q_ref[h, :, :LANES] = (qf[:, c0:c0 + LANES] * scale).astype(BF16)
        q_ref[h, :, LANES:] = (_rope_block(qf[:, c0 + LANES:c0 + QK_WIDTH], cosb, sinb) * scale).astype(BF16)
        k_ref[h, :, :LANES] = kvf[:, c0:c0 + LANES].astype(BF16)
        k_ref[h, :, LANES:] = kpe_r
        v_ref[h] = kvf[:, c0 + LANES:c0 + QK_WIDTH].astype(BF16)


def _mla_prep(proj, qg, kvg, wuq, wukv, cosb, sinb):
    B, S, _ = proj.shape
    H = MLA_HEADS
    tm = _tile(S, 512)
    return pl.pallas_call(
        _mla_prep_kernel,
        grid=(B, S // tm),
        in_specs=[
            pl.BlockSpec((None, tm, PROJ_GROUP), lambda b, i: (b, i, 0)),
            pl.BlockSpec((1, MLA_Q_RANK), lambda b, i: (0, 0)),
            pl.BlockSpec((1, MLA_KV_RANK), lambda b, i: (0, 0)),
            pl.BlockSpec((MLA_Q_RANK, H * QK_WIDTH), lambda b, i: (0, 0)),
            pl.BlockSpec((MLA_KV_RANK, H * QK_WIDTH), lambda b, i: (0, 0)),
            pl.BlockSpec((tm, LANES), lambda b, i: (i, 0)),
            pl.BlockSpec((tm, LANES), lambda b, i: (i, 0)),
        ],
        out_specs=[
            pl.BlockSpec((None, H, tm, QK_WIDTH), lambda b, i: (b, 0, i, 0)),
            pl.BlockSpec((None, H, tm, QK_WIDTH), lambda b, i: (b, 0, i, 0)),
            pl.BlockSpec((None, H, tm, MLA_V), lambda b, i: (b, 0, i, 0)),
        ],
        out_shape=[
            jax.ShapeDtypeStruct((B, H, S, QK_WIDTH), BF16),
            jax.ShapeDtypeStruct((B, H, S, QK_WIDTH), BF16),
            jax.ShapeDtypeStruct((B, H, S, MLA_V), BF16),
        ],
        compiler_params=_params(("arbitrary", "arbitrary")),
        name="mla_prep",
    )(proj, qg, kvg, wuq, wukv, cosb, sinb)


_NT = (((1,), (1,)), ((), ()))
_TN = (((0,), (0,)), ((), ()))


def _attn_kernel(q_ref, k_ref, v_ref, o_ref, *, tq, tk, hp):
    i = pl.program_id(2)
    sub = tq // tk

    def scores(h, j):
        start = pl.multiple_of(j * tk, tk)
        return lax.dot_general(q_ref[h], k_ref[h, pl.ds(start, tk), :], _NT, preferred_element_type=F32)

    def update(h, j, s, m, l, acc, masked_sub):
        if masked_sub is not None:
            qc = jnp.right_shift(lax.broadcasted_iota(I32, (tq, tk), 0), CHUNK_SHIFT)
            kc = jnp.right_shift(lax.broadcasted_iota(I32, (tq, tk), 1) + masked_sub * tk, CHUNK_SHIFT)
            s = jnp.where(kc <= qc, s, NEG_BIG)
        m_new = jnp.maximum(m, jnp.max(s, axis=-1, keepdims=True))
        alpha = jnp.exp2(m - m_new)
        p = jnp.exp2(s - m_new)
        l = alpha * l + jnp.sum(p, axis=-1, keepdims=True)
        start = pl.multiple_of(j * tk, tk)
        acc = alpha * acc + jnp.dot(p.astype(BF16), v_ref[h, pl.ds(start, tk), :], preferred_element_type=F32)
        return m_new, l, acc

    def body(j, carry):
        return tuple(update(h, j, scores(h, j), *carry[h], None) for h in range(hp))

    def body2(j2, carry):
        out = []
        for h in range(hp):
            c = update(h, 2 * j2, scores(h, 2 * j2), *carry[h], None)
            out.append(update(h, 2 * j2 + 1, scores(h, 2 * j2 + 1), *c, None))
        return tuple(out)

    init = tuple((jnp.full((tq, 1), NEG_BIG, F32), jnp.zeros((tq, 1), F32), jnp.zeros((tq, MLA_V), F32))
                 for h in range(hp))
    n_full = i * sub
    carry = lax.fori_loop(0, n_full // 2, body2, init)
    carry = lax.fori_loop(2 * (n_full // 2), n_full, body, carry)
    for h in range(hp):
        m, l, acc = carry[h]
        for jj in range(sub):
            m, l, acc = update(h, n_full + jj, scores(h, n_full + jj), m, l, acc, jj)
        o_ref[:, h * MLA_V:(h + 1) * MLA_V] = (acc / l).astype(o_ref.dtype)


def _attention(q, k, v):
    B, H, S, _ = q.shape
    tq = _tile(S, 512)
    tk = _tile(tq, 512)
    hp = 4
    return pl.pallas_call(
        functools.partial(_attn_kernel, tq=tq, tk=tk, hp=hp),
        grid=(B, H // hp, S // tq),
        in_specs=[
            pl.BlockSpec((None, hp, tq, QK_WIDTH), lambda b, h, i: (b, h, i, 0)),
            pl.BlockSpec((None, hp, S, QK_WIDTH), lambda b, h, i: (b, h, 0, 0)),
            pl.BlockSpec((None, hp, S, MLA_V), lambda b, h, i: (b, h, 0, 0)),
        ],
        out_specs=pl.BlockSpec((None, tq, hp * MLA_V), lambda b, h, i: (b, i, h)),
        out_shape=jax.ShapeDtypeStruct((B, S, H * MLA_V), BF16),
        compiler_params=_params(("arbitrary", "arbitrary", "arbitrary")),
        name="attention",
    )(q, k, v)


def _retention_kernel(q_ref, k_ref, v_ref, g_ref, cos_ref, sin_ref, o_ref,
                      state_ref, decay_ref, qdec_ref, kdec_ref, *, tb):
    first_block = pl.program_id(1) == 0
    log_g = [float(np.log1p(-np.exp2(-5.0 - h))) for h in range(RET_HEADS)]

    @pl.when(jnp.logical_and(pl.program_id(0) == 0, first_block))
    def _():
        rel = (lax.broadcasted_iota(I32, (tb, tb), 0) - lax.broadcasted_iota(I32, (tb, tb), 1)).astype(F32)
        idx = lax.broadcasted_iota(I32, (tb, RET_DK), 0).astype(F32)
        for h in range(RET_HEADS):
            decay_ref[h] = jnp.where(rel >= 0, jnp.exp(jnp.maximum(rel, 0.0) * log_g[h]), 0.0)
            qdec_ref[h] = jnp.exp((idx + 1.0) * log_g[h])
            kdec_ref[h] = jnp.exp((tb - 1.0 - idx) * log_g[h])

    @pl.when(first_block)
    def _():
        state_ref[...] = jnp.zeros_like(state_ref)

    cosr = cos_ref[...]
    sinr = sin_ref[...]
    for h in range(RET_HEADS):
        sl = slice(h * RET_DK, (h + 1) * RET_DK)
        q = _rope_block(q_ref[:, sl].astype(F32), cosr, sinr)
        k = _rope_block(k_ref[:, sl].astype(F32), cosr, sinr) * (RET_DK ** -0.5)
        vb = v_ref[:, sl]
        scores = lax.dot_general(q.astype(BF16), k.astype(BF16), _NT, preferred_element_type=F32) * decay_ref[h]
        y = jnp.dot(scores.astype(BF16), vb, preferred_element_type=F32)
        state = state_ref[h]
        q_dec = q * qdec_ref[h]
        y = y + jnp.dot(q_dec.astype(BF16), state.astype(BF16), preferred_element_type=F32)
        k_dec = k * kdec_ref[h]
        upd = lax.dot_general(k_dec.astype(BF16), vb, _TN, preferred_element_type=F32)
        state_ref[h] = state * float(np.exp(tb * log_g[h])) + upd
        mu = jnp.mean(y, axis=-1, keepdims=True)
        yc = y - mu
        var = jnp.mean(yc * yc, axis=-1, keepdims=True)
        g = g_ref[:, sl].astype(F32)
        o_ref[:, sl] = (g * jax.nn.sigmoid(g) * (yc * lax.rsqrt(var + EPS))).astype(o_ref.dtype)


def _retention(proj, cosr, sinr):
    B, S, _ = proj.shape
    tb = _tile(S, 256)
    W = RET_HEADS * RET_DK
    col = lambda g: pl.BlockSpec((None, tb, W), lambda b, i: (b, i, g))
    return pl.pallas_call(
        functools.partial(_retention_kernel, tb=tb),
        grid=(B, S // tb),
        in_specs=[col(1), col(2), col(3), col(4),
                  pl.BlockSpec((tb, LANES), lambda b, i: (i, 0)),
                  pl.BlockSpec((tb, LANES), lambda b, i: (i, 0))],
        out_specs=pl.BlockSpec((None, tb, W), lambda b, i: (b, i, 0)),
        out_shape=jax.ShapeDtypeStruct((B, S, W), BF16),
        scratch_shapes=[pltpu.VMEM((RET_HEADS, RET_DK, RET_DV), F32),
                        pltpu.VMEM((RET_HEADS, tb, tb), F32),
                        pltpu.VMEM((RET_HEADS, tb, RET_DK), F32),
                        pltpu.VMEM((RET_HEADS, tb, RET_DK), F32)],
        compiler_params=_params(("arbitrary", "arbitrary")),
        name="retention",
    )(proj, proj, proj, proj, cosr, sinr)


def _oproj_kernel(ym_ref, yr_ref, wo_ref, x_ref, g1_ref, a2_ref, b2_ref, x1_ref, h2_ref, *, tm, ts):
    for r0 in range(0, tm, ts):
        rows = pl.ds(r0, ts)
        acc = jnp.dot(ym_ref[rows, :], wo_ref[:MIX_HALF, :], preferred_element_type=F32)
        acc = acc + jnp.dot(yr_ref[rows, :], wo_ref[MIX_HALF:, :], preferred_element_type=F32)
        x1 = x_ref[rows, :] + g1_ref[...] * acc
        x1_ref[rows, :] = x1
        h2_ref[rows, :] = (x1 * lax.rsqrt(jnp.mean(x1 * x1, axis=-1, keepdims=True) + EPS) * a2_ref[...]
                           + b2_ref[...])


def _oproj(y_mla, y_ret, wo, x, g1, a2, b2):
    B, S, D = x.shape
    tm = _tile(S, 512)
    row = lambda w: pl.BlockSpec((None, tm, w), lambda b, i: (b, i, 0))
    vec = pl.BlockSpec((None, 1, D), lambda b, i: (b, 0, 0))
    return pl.pallas_call(
        functools.partial(_oproj_kernel, tm=tm, ts=_tile(tm, 256)),
        grid=(B, S // tm),
        in_specs=[row(MIX_HALF), row(MIX_HALF),
                  pl.BlockSpec(wo.shape, lambda b, i: (0, 0), pipeline_mode=pl.Buffered(1)),
                  row(D), vec, vec, vec],
        out_specs=[row(D), row(D)],
        out_shape=[jax.ShapeDtypeStruct((B, S, D), F32), jax.ShapeDtypeStruct((B, S, D), F32)],
        compiler_params=_params(("arbitrary", "arbitrary")),
        name="oproj",
    )(y_mla, y_ret, wo, x, g1, a2, b2)


ROUTE_ROWS = 8


def _router_kernel(h2_ref, wr_ref, br_ref, ri_ref, rw_ref, cnt_ref, carry_ref, *, tr):
    @pl.when(pl.program_id(0) == 0)
    def _():
        carry_ref[...] = jnp.zeros_like(carry_ref)

    h2 = h2_ref[...]
    h_hi = h2.astype(BF16)
    h_lo = (h2 - h_hi.astype(F32)).astype(BF16)
    lg2 = (jnp.dot(h_hi, wr_ref[...], preferred_element_type=F32)
           + jnp.dot(h_lo, wr_ref[...], preferred_element_type=F32))
    lg = lg2[:, :LANES] + lg2[:, LANES:] + br_ref[...]

    lane = lax.broadcasted_iota(I32, (tr, LANES), 1)
    big = jnp.int32(1 << 20)
    is_g = lane < N_GROUPS
    e_lane = lane - N_GROUPS
    is_e = jnp.logical_and(e_lane >= 0, e_lane < N_EXPERTS)
    gl = jnp.where(is_g, lg, NEG_BIG)
    gmax = jnp.max(gl, axis=-1, keepdims=True)
    gsel = jnp.min(jnp.where(gl == gmax, lane, big), axis=-1, keepdims=True)
    p_group = 1.0 / jnp.sum(jnp.where(is_g, jnp.exp(gl - gmax), 0.0), axis=-1, keepdims=True)
    in_grp = jnp.logical_and(is_e, jnp.right_shift(e_lane, EPG_SHIFT) == gsel)
    el = jnp.where(in_grp, lg, NEG_BIG)
    t1 = jnp.max(el, axis=-1, keepdims=True)
    i1 = jnp.min(jnp.where(el == t1, lane, big), axis=-1, keepdims=True)
    el2 = jnp.where(lane == i1, NEG_BIG, el)
    t2 = jnp.max(el2, axis=-1, keepdims=True)
    i2 = jnp.min(jnp.where(el2 == t2, lane, big), axis=-1, keepdims=True)
    ex = jnp.exp(t2 - t1)
    w1 = p_group / (1.0 + ex)
    w2 = p_group * ex / (1.0 + ex)

    oh1 = lane == i1
    oh2 = lane == i2
    cnt = jnp.where(jnp.logical_or(oh1, oh2), 1.0, 0.0)
    strict_lower = jnp.where(lax.broadcasted_iota(I32, (tr, tr), 1) < lax.broadcasted_iota(I32, (tr, tr), 0),
                             1.0, 0.0).astype(BF16)
    before = jnp.dot(strict_lower, cnt.astype(BF16), preferred_element_type=F32) + carry_ref[...]
    r1 = jnp.sum(jnp.where(oh1, before, 0.0), axis=-1, keepdims=True)
    r2 = jnp.sum(jnp.where(oh2, before, 0.0), axis=-1, keepdims=True)
    carry_ref[...] = carry_ref[...] + jnp.sum(cnt, axis=0, keepdims=True)
    cnt_ref[...] = carry_ref[...]

    rec = jnp.where(lane == 0, (i1 - N_GROUPS).astype(F32), jnp.where(lane == 1, (i2 - N_GROUPS).astype(F32),
                    jnp.where(lane == 2, r1, jnp.where(lane == 3, r2, 0.0))))
    ri_ref[...] = jnp.transpose(rec)[:ROUTE_ROWS, :]
    rw_ref[...] = jnp.where(lane == 0, w1, jnp.where(lane == 1, w2, 0.0))


def _router(h2, wr, br):
    T, D = h2.shape
    tr = _tile(T, 1024)
    return pl.pallas_call(
        functools.partial(_router_kernel, tr=tr),
        grid=(T // tr,),
        in_specs=[pl.BlockSpec((tr, D), lambda i: (i, 0)),
                  pl.BlockSpec(wr.shape, lambda i: (0, 0)),
                  pl.BlockSpec(br.shape, lambda i: (0, 0))],
        out_specs=[pl.BlockSpec((ROUTE_ROWS, tr), lambda i: (0, i)),
                   pl.BlockSpec((tr, LANES), lambda i: (i, 0)),
                   pl.BlockSpec((1, LANES), lambda i: (0, 0))],
        out_shape=[jax.ShapeDtypeStruct((ROUTE_ROWS, T), F32),
                   jax.ShapeDtypeStruct((T, LANES), F32),
                   jax.ShapeDtypeStruct((1, LANES), F32)],
        scratch_shapes=[pltpu.VMEM((1, LANES), F32)],
        compiler_params=_params(("arbitrary",)),
        name="router",
    )(h2, wr, br)


def _dispatch_kernel(pos_ref, zt_ref, h_ref, xs_ref, zero_ref, sem, zsem, *, tm, n_tok, te):
    base = pl.program_id(0) * tm

    @pl.when(pl.program_id(0) == 0)
    def _():
        zero_ref[...] = jnp.zeros_like(zero_ref)

        def zero_copy(n):
            row = pl.multiple_of(zt_ref[n] * te, te)
            return pltpu.make_async_copy(zero_ref, xs_ref.at[pl.ds(row, te)], zsem)

        for n in range(zt_ref.shape[0]):
            @pl.when(zt_ref[n] >= 0)
            def _():
                zero_copy(n).start()
        for n in range(zt_ref.shape[0]):
            @pl.when(zt_ref[n] >= 0)
            def _():
                zero_copy(n).wait()

    def row_copy(r, p):
        return pltpu.make_async_copy(h_ref.at[pl.ds(r, 1)], xs_ref.at[pl.ds(p, 1)], sem)

    def issue(r, c):
        row_copy(r, pos_ref[base + r]).start()
        row_copy(r, pos_ref[n_tok + base + r]).start()
        return c

    lax.fori_loop(0, tm, issue, 0, unroll=8)
    for _ in range(2):
        pltpu.make_async_copy(h_ref, xs_ref.at[pl.ds(0, tm)], sem).wait()


def _dispatch(pos, zero_tiles, h2, n_rows, te):
    T, D = h2.shape
    tm = _tile(T, 512)
    grid_spec = pltpu.PrefetchScalarGridSpec(
        num_scalar_prefetch=2,
        grid=(T // tm,),
        in_specs=[pl.BlockSpec((tm, D), lambda i, pos, zt: (i, 0))],
        out_specs=pl.BlockSpec(memory_space=pl.ANY),
        scratch_shapes=[pltpu.VMEM((te, D), F32), pltpu.SemaphoreType.DMA(()), pltpu.SemaphoreType.DMA(())],
    )
    return pl.pallas_call(
        functools.partial(_dispatch_kernel, tm=tm, n_tok=T, te=te),
        grid_spec=grid_spec,
        out_shape=jax.ShapeDtypeStruct((n_rows, D), F32),
        compiler_params=_params(("arbitrary",)),
        name="dispatch",
    )(pos, zero_tiles, h2)


def _expert_kernel(te_ref, nx_ref, nt_ref, xs_ref, wg_hbm, wu_hbm, wd_hbm, ys_ref,
                   wgf_ref, wuf_ref, wdf_ref, wgb_ref, wub_ref, wdb_ref, sem, *, layer):
    i = pl.program_id(0)
    used = i < nt_ref[0]
    expert = te_ref[i]
    new_expert = jnp.logical_or(i == 0, expert != te_ref[jnp.maximum(i - 1, 0)])

    def fetch(e):
        return (pltpu.make_async_copy(wg_hbm.at[layer, e], wgf_ref, sem.at[0]),
                pltpu.make_async_copy(wu_hbm.at[layer, e], wuf_ref, sem.at[1]),
                pltpu.make_async_copy(wd_hbm.at[layer, e], wdf_ref, sem.at[2]))

    @pl.when(i == 0)
    def _():
        for c in fetch(expert):
            c.start()

    @pl.when(jnp.logical_and(used, new_expert))
    def _():
        for c in fetch(expert):
            c.wait()
        wgb_ref[...] = wgf_ref[...].astype(BF16)
        wub_ref[...] = wuf_ref[...].astype(BF16)
        wdb_ref[...] = wdf_ref[...].astype(BF16)

        @pl.when(nx_ref[i] >= 0)
        def _():
            for c in fetch(nx_ref[i]):
                c.start()

    @pl.when(jnp.logical_not(used))
    def _():
        ys_ref[...] = jnp.zeros_like(ys_ref)

    @pl.when(used)
    def _():
        xb = xs_ref[...].astype(BF16)
        a = jnp.dot(xb, wgb_ref[...], preferred_element_type=F32)
        u = jnp.dot(xb, wub_ref[...], preferred_element_type=F32)
        hid = (a * jax.nn.sigmoid(a) * u).astype(BF16)
        ys_ref[...] = jnp.dot(hid, wdb_ref[...], preferred_element_type=F32)


def _experts(tile_expert, next_expert, n_tiles, xs, wg, wu, wd, layer, te):
    P, D = xs.shape
    F = wg.shape[-1]
    row_idx = lambda i, te_ref, nx_ref, nt_ref: (jnp.minimum(i, nt_ref[0] - 1), 0)
    hbm = pl.BlockSpec(memory_space=pl.ANY)
    grid_spec = pltpu.PrefetchScalarGridSpec(
        num_scalar_prefetch=3,
        grid=(P // te,),
        in_specs=[pl.BlockSpec((te, D), row_idx), hbm, hbm, hbm],
        out_specs=pl.BlockSpec((te, D), lambda i, te_ref, nx_ref, nt_ref: (i, 0)),
        scratch_shapes=[pltpu.VMEM((D, F), F32), pltpu.VMEM((D, F), F32), pltpu.VMEM((F, D), F32),
                        pltpu.VMEM((D, F), BF16), pltpu.VMEM((D, F), BF16), pltpu.VMEM((F, D), BF16),
                        pltpu.SemaphoreType.DMA((3,))],
    )
    return pl.pallas_call(
        functools.partial(_expert_kernel, layer=layer),
        grid_spec=grid_spec,
        out_shape=jax.ShapeDtypeStruct((P, D), F32),
        compiler_params=_params(("arbitrary",)),
        name="experts",
    )(tile_expert, next_expert, n_tiles, xs, wg, wu, wd)


def _combine_kernel(pos_ref, x1_ref, rw_ref, g2_ref, fg_ref, ys_ref, o_ref, buf_ref, sem, *, tm, n_tok, seq, final):
    base = pl.program_id(0) * seq + pl.program_id(1) * tm

    def row_copy(k, r, p):
        return pltpu.make_async_copy(ys_ref.at[pl.ds(p, 1)], buf_ref.at[k, pl.ds(r, 1)], sem)

    def issue(r, c):
        row_copy(0, r, pos_ref[base + r]).start()
        row_copy(1, r, pos_ref[n_tok + base + r]).start()
        return c

    lax.fori_loop(0, tm, issue, 0, unroll=8)
    for k in range(2):
        pltpu.make_async_copy(ys_ref.at[pl.ds(0, tm)], buf_ref.at[k], sem).wait()
    w = rw_ref[...]
    moe = w[:, 0:1] * buf_ref[0] + w[:, 1:2] * buf_ref[1]
    x2 = x1_ref[...] + g2_ref[...] * moe
    if final:
        x2 = x2 * lax.rsqrt(jnp.mean(x2 * x2, axis=-1, keepdims=True) + EPS) * fg_ref[...]
    o_ref[...] = x2


def _combine(pos, x1, rw, g2, fg, ys, final):
    B, S, D = x1.shape
    tm = _tile(S, 512)
    grid_spec = pltpu.PrefetchScalarGridSpec(
        num_scalar_prefetch=1,
        grid=(B, S // tm),
        in_specs=[pl.BlockSpec((None, tm, D), lambda b, i, pos: (b, i, 0)),
                  pl.BlockSpec((None, tm, LANES), lambda b, i, pos: (b, i, 0)),
                  pl.BlockSpec((None, 1, D), lambda b, i, pos: (b, 0, 0)),
                  pl.BlockSpec((1, D), lambda b, i, pos: (0, 0)),
                  pl.BlockSpec(memory_space=pl.ANY)],
        out_specs=pl.BlockSpec((None, tm, D), lambda b, i, pos: (b, i, 0)),
        scratch_shapes=[pltpu.VMEM((2, tm, D), F32), pltpu.SemaphoreType.DMA(())],
    )
    return pl.pallas_call(
        functools.partial(_combine_kernel, tm=tm, n_tok=B * S, seq=S, final=final),
        grid_spec=grid_spec,
        out_shape=jax.ShapeDtypeStruct((B, S, D), F32),
        compiler_params=_params(("arbitrary", "arbitrary")),
        name="combine",
    )(pos, x1, rw, g2, fg, ys)


def _rope_tables(S, half):
    inv = ROPE_BASE ** (-jnp.arange(half, dtype=F32) / half)
    ang = jnp.arange(S, dtype=jnp.int32).astype(F32)[:, None] * inv[None, :]
    cos, sin = jnp.cos(ang), jnp.sin(ang)
    z = jnp.zeros((S, LANES // 2 - half), F32)
    return (jnp.concatenate([cos, z, cos, z], axis=-1), jnp.concatenate([-sin, z, sin, z], axis=-1))


def _layout_w_in(w_in):
    D = w_in.shape[0]
    o = np.cumsum([0, MLA_Q_RANK, MLA_KV_RANK, MLA_ROPE, 1024, 1024, 1024, 1024])
    half = MLA_ROPE // 2
    z = lambda n: jnp.zeros((D, n), w_in.dtype)
    kpe = w_in[:, o[2]:o[3]]
    cols = [w_in[:, o[0]:o[2]], kpe[:, :half], z(LANES // 2 - half), kpe[:, half:], z(LANES // 2 - half),
            z(PROJ_GROUP - MLA_Q_RANK - MLA_KV_RANK - LANES), w_in[:, o[3]:o[7]]]
    return jnp.concatenate(cols, axis=-1).astype(BF16)


def _layout_w_uq(w_uq):
    R = w_uq.shape[0]
    half = MLA_ROPE // 2
    w = w_uq.reshape(R, MLA_HEADS, MLA_NOPE + MLA_ROPE)
    z = jnp.zeros((R, MLA_HEADS, LANES // 2 - half), w_uq.dtype)
    w = jnp.concatenate([w[..., :MLA_NOPE], w[..., MLA_NOPE:MLA_NOPE + half], z, w[..., MLA_NOPE + half:], z], axis=-1)
    return w.reshape(R, MLA_HEADS * QK_WIDTH).astype(BF16)


def kernel(x, c, ada_w, ada_b, norm1_g, w_in, q_norm_g, w_uq, kv_norm_g, w_ukv, w_o, norm2_g, router_group_w, router_group_b, router_expert_w, router_expert_b, w_gate, w_up, w_down, final_norm_g):
    B, S, D = x.shape
    L = ada_w.shape[0]
    T = B * S
    te = 256
    n_rows = 2 * T + N_EXPERTS * te
    n_tiles_max = n_rows // te

    mod = _adaln_mod(c, ada_w, ada_b)
    cos_m, sin_m = _rope_tables(S, MLA_ROPE // 2)
    cos_r, sin_r = _rope_tables(S, RET_DK // 2)
    fg = final_norm_g.reshape(1, D)

    for l in range(L):
        sh1, sc1, g1, sh2, sc2, g2 = [mod[l, :, i * D:(i + 1) * D][:, None, :] for i in range(6)]
        a1 = norm1_g[l][None, None, :] * (1.0 + sc1)
        a2 = norm2_g[l][None, None, :] * (1.0 + sc2)

        proj = _norm_matmul(x, a1, sh1, _layout_w_in(w_in[l]))
        q, k, v = _mla_prep(proj, q_norm_g[l].reshape(1, -1), kv_norm_g[l].reshape(1, -1),
                            _layout_w_uq(w_uq[l]), w_ukv[l].astype(BF16), cos_m, sin_m)
        y_mla = _attention(q, k, v)
        y_ret = _retention(proj, cos_r, sin_r)

        wr = jnp.concatenate([router_group_w[l], router_expert_w[l],
                              jnp.zeros((D, LANES - N_GROUPS - N_EXPERTS), F32)], axis=-1)
        br = jnp.concatenate([router_group_b[l], router_expert_b[l],
                              jnp.zeros((LANES - N_GROUPS - N_EXPERTS,), F32)]).reshape(1, LANES)
        wr_hi = wr.astype(BF16)
        wr_lo = (wr - wr_hi.astype(F32)).astype(BF16)
        x1, h2 = _oproj(y_mla, y_ret, w_o[l].astype(BF16), x, g1, a2, sh2)
        h2 = h2.reshape(T, D)
        ri, rw, cnt = _router(h2, jnp.concatenate([wr_hi, wr_lo], axis=-1), br)
        rw = rw.reshape(B, S, LANES)

        counts = cnt[0, N_GROUPS:N_GROUPS + N_EXPERTS].astype(I32)
        padded = ((counts + te - 1) // te) * te
        ends = jnp.cumsum(padded)
        off = ends - padded
        ri = ri.astype(I32)
        pos = jnp.concatenate([off[ri[0]] + ri[2], off[ri[1]] + ri[3]]).astype(I32)
        n_tiles = (ends[-1] // te).astype(I32)
        tile_ids = jnp.minimum(jnp.arange(n_tiles_max, dtype=I32), n_tiles - 1)
        tile_expert = jnp.sum((ends // te)[None, :] <= tile_ids[:, None], axis=1).astype(I32)
        spare = n_tiles + jnp.arange(n_tiles_max - 2 * T // te, dtype=I32)
        zero_tiles = jnp.concatenate([jnp.where(padded > counts, ends // te - 1, -1),
                                      jnp.where(spare < n_tiles_max, spare, -1)]).astype(I32)

        xs = _dispatch(pos, zero_tiles, h2, n_rows, te)
        eids = jnp.arange(N_EXPERTS, dtype=I32)
        later = jnp.logical_and(eids[None, :] > eids[:, None], (padded > 0)[None, :])
        next_of = jnp.min(jnp.where(later, eids[None, :], N_EXPERTS), axis=1)
        next_of = jnp.where(next_of < N_EXPERTS, next_of, -1)
        next_expert = jnp.sum(jnp.where(tile_expert[:, None] == eids[None, :], next_of[None, :], 0), axis=1).astype(I32)
        ys = _experts(tile_expert, next_expert, n_tiles.reshape(1), xs, w_gate, w_up, w_down, l, te)
        x = _combine(pos, x1, rw, g2, fg, ys, final=(l == L - 1))
    return x
```

```python
import functools

import numpy as np
import jax
import jax.numpy as jnp
from jax import lax
from jax.experimental import pallas as pl
from jax.experimental.pallas import tpu as pltpu

F32 = jnp.float32
BF16 = jnp.bfloat16
I32 = jnp.int32
EPS = 1e-6
ROPE_BASE = 10000.0
CHUNK = 64

MLA_HEADS = 8
MLA_Q_RANK = 512
MLA_KV_RANK = 256
MLA_NOPE = 128
MLA_ROPE = 64
MLA_V = 128
RET_HEADS = 8
RET_DK = 128
RET_DV = 128
N_GROUPS = 4
EXPERTS_PER_GROUP = 8
N_EXPERTS = N_GROUPS * EXPERTS_PER_GROUP
CHUNK_SHIFT = CHUNK.bit_length() - 1
EPG_SHIFT = EXPERTS_PER_GROUP.bit_length() - 1

LANES = 128
QK_WIDTH = 2 * LANES
MIX_HALF = MLA_HEADS * MLA_V
PROJ_GROUP = 1024
PROJ_WIDTH = 5 * PROJ_GROUP
NEG_BIG = -1e30
VMEM_LIMIT = 56 * 1024 * 1024


def _tile(n, pref):
    if n <= pref:
        return n
    t = pref - pref % LANES
    while n % t:
        t -= LANES
    assert t > 0, (n, pref)
    return t


def _params(sem, vmem=VMEM_LIMIT):
    return pltpu.CompilerParams(dimension_semantics=sem, vmem_limit_bytes=vmem)


def _mod_kernel(c_ref, w_ref, b_ref, o_ref):
    c = c_ref[...]
    s = (c * jax.nn.sigmoid(c)).astype(BF16)
    o_ref[...] = jnp.dot(s, w_ref[...].astype(BF16), preferred_element_type=F32) + b_ref[...]


def _adaln_mod(c, ada_w, ada_b):
    B, D = c.shape
    L, _, N = ada_w.shape
    bp = 8
    cp = jnp.pad(c, ((0, bp - B), (0, 0)))
    tn = _tile(N, 1024)
    out = pl.pallas_call(
        _mod_kernel,
        grid=(L, N // tn),
        in_specs=[
            pl.BlockSpec((bp, D), lambda l, j: (0, 0)),
            pl.BlockSpec((None, D, tn), lambda l, j: (l, 0, j)),
            pl.BlockSpec((None, 1, tn), lambda l, j: (l, 0, j)),
        ],
        out_specs=pl.BlockSpec((None, bp, tn), lambda l, j: (l, 0, j)),
        out_shape=jax.ShapeDtypeStruct((L, bp, N), F32),
        compiler_params=_params(("arbitrary", "arbitrary")),
        name="adaln_mod",
    )(cp, ada_w, ada_b.reshape(L, 1, N))
    return out[:, :B]


def _norm_matmul_kernel(x_ref, a_ref, b_ref, wa_ref, wb_ref, o_ref, h_ref):
    j = pl.program_id(2)

    @pl.when(j == 0)
    def _():
        x = x_ref[...]
        ms = jnp.mean(x * x, axis=-1, keepdims=True)
        h_ref[...] = (x * lax.rsqrt(ms + EPS) * a_ref[...] + b_ref[...]).astype(BF16)
        o_ref[...] = jnp.dot(h_ref[...], wa_ref[...], preferred_element_type=F32).astype(o_ref.dtype)

    @pl.when(j > 0)
    def _():
        o_ref[...] = jnp.dot(h_ref[...], wb_ref[...], preferred_element_type=F32).astype(o_ref.dtype)


def _norm_matmul(x, a, b, wa, wb, layer):
    B, S, D = x.shape
    tn = PROJ_GROUP
    nb = wb.shape[-1] // tn
    tm = _tile(S, 1024)
    return pl.pallas_call(
        _norm_matmul_kernel,
        grid=(B, S // tm, 1 + nb),
        in_specs=[
            pl.BlockSpec((None, tm, D), lambda b, i, j: (b, i, 0)),
            pl.BlockSpec((None, 1, D), lambda b, i, j: (b, 0, 0)),
            pl.BlockSpec((None, 1, D), lambda b, i, j: (b, 0, 0)),
            pl.BlockSpec((None, D, tn), lambda b, i, j: (layer, 0, 0)),
            pl.BlockSpec((None, D, tn), lambda b, i, j: (layer, 0, jnp.maximum(j - 1, 0))),
        ],
        out_specs=pl.BlockSpec((None, tm, tn), lambda b, i, j: (b, i, j)),
        out_shape=jax.ShapeDtypeStruct((B, S, (1 + nb) * tn), BF16),
        scratch_shapes=[pltpu.VMEM((tm, D), BF16)],
        compiler_params=_params(("arbitrary", "arbitrary", "arbitrary")),
        name="norm_matmul",
    )(x, a, b, wa, wb)


def _rms(x, g):
    return x * lax.rsqrt(jnp.mean(x * x, axis=-1, keepdims=True) + EPS) * g


def _rope_block(x, cosb, sinb):
    return x * cosb + pltpu.roll(x, LANES // 2, 1) * sinb


def _mla_prep_kernel(p_ref, qg_ref, kvg_ref, wuq_ref, wukv_ref, cos_ref, sin_ref, q_ref, k_ref, v_ref):
    p = p_ref[...].astype(F32)
    cq = p[:, :MLA_Q_RANK]
    ckv = p[:, MLA_Q_RANK:MLA_Q_RANK + MLA_KV_RANK]
    kpe = p[:, MLA_Q_RANK + MLA_KV_RANK:MLA_Q_RANK + MLA_KV_RANK + LANES]
    cosb = cos_ref[...]
    sinb = sin_ref[...]
    scale = (MLA_NOPE + MLA_ROPE) ** -0.5 * float(np.log2(np.e))
    qf = jnp.dot(_rms(cq, qg_ref[...]).astype(BF16), wuq_ref[...], preferred_element_type=F32)
    kvf = jnp.dot(_rms(ckv, kvg_ref[...]).astype(BF16), wukv_ref[...], preferred_element_type=F32)
    kpe_r = _rope_block(kpe, cosb, sinb).astype(BF16)
    for h in range(MLA_HEADS):
        c0 = h * QK_WIDTH
        q_ref[h, :, :LANES] = (qf[:, c0:c0 + LANES] * scale).astype(BF16)
        q_ref[h, :, LANES:] = (_rope_block(qf[:, c0 + LANES:c0 + QK_WIDTH], cosb, sinb) * scale).astype(BF16)
        k_ref[h, :, :LANES] = kvf[:, c0:c0 + LANES].astype(BF16)
        k_ref[h, :, LANES:] = kpe_r
        v_ref[h] = kvf[:, c0 + LANES:c0 + QK_WIDTH].astype(BF16)


def _mla_prep(proj, qg, kvg, wuq, wukv, cosb, sinb, layer):
    B, S, _ = proj.shape
    H = MLA_HEADS
    tm = _tile(S, 512)
    return pl.pallas_call(
        _mla_prep_kernel,
        grid=(B, S // tm),
        in_specs=[
            pl.BlockSpec((None, tm, PROJ_GROUP), lambda b, i: (b, i, 0)),
            pl.BlockSpec((None, 1, MLA_Q_RANK), lambda b, i: (layer, 0, 0)),
            pl.BlockSpec((None, 1, MLA_KV_RANK), lambda b, i: (layer, 0, 0)),
            pl.BlockSpec((None, MLA_Q_RANK, H * QK_WIDTH), lambda b, i: (layer, 0, 0)),
            pl.BlockSpec((None, MLA_KV_RANK, H * QK_WIDTH), lambda b, i: (layer, 0, 0)),
            pl.BlockSpec((tm, LANES), lambda b, i: (i, 0)),
            pl.BlockSpec((tm, LANES), lambda b, i: (i, 0)),
        ],
        out_specs=[
            pl.BlockSpec((None, H, tm, QK_WIDTH), lambda b, i: (b, 0, i, 0)),
            pl.BlockSpec((None, H, tm, QK_WIDTH), lambda b, i: (b, 0, i, 0)),
            pl.BlockSpec((None, H, tm, MLA_V), lambda b, i: (b, 0, i, 0)),
        ],
        out_shape=[
            jax.ShapeDtypeStruct((B, H, S, QK_WIDTH), BF16),
            jax.ShapeDtypeStruct((B, H, S, QK_WIDTH), BF16),
            jax.ShapeDtypeStruct((B, H, S, MLA_V), BF16),
        ],
        compiler_params=_params(("arbitrary", "arbitrary")),
        name="mla_prep",
    )(proj, qg, kvg, wuq, wukv, cosb, sinb)


_NT = (((1,), (1,)), ((), ()))
_TN = (((0,), (0,)), ((), ()))


def _attn_kernel(q_ref, k_ref, v_ref, o_ref, *, tq, tk, hp):
    i = pl.program_id(2)
    sub = tq // tk

    def scores(h, j):
        start = pl.multiple_of(j * tk, tk)
        return lax.dot_general(q_ref[h], k_ref[h, pl.ds(start, tk), :], _NT, preferred_element_type=F32)

    def update(h, j, s, m, l, acc, masked_sub):
        if masked_sub is not None:
            qc = jnp.right_shift(lax.broadcasted_iota(I32, (tq, tk), 0), CHUNK_SHIFT)
            kc = jnp.right_shift(lax.broadcasted_iota(I32, (tq, tk), 1) + masked_sub * tk, CHUNK_SHIFT)
            s = jnp.where(kc <= qc, s, NEG_BIG)
        m_new = jnp.maximum(m, jnp.max(s, axis=-1, keepdims=True))
        alpha = jnp.exp2(m - m_new)
        p = jnp.exp2(s - m_new)
        l = alpha * l + jnp.sum(p, axis=-1, keepdims=True)
        start = pl.multiple_of(j * tk, tk)
        acc = alpha * acc + jnp.dot(p.astype(BF16), v_ref[h, pl.ds(start, tk), :], preferred_element_type=F32)
        return m_new, l, acc

    def body(j, carry):
        return tuple(update(h, j, scores(h, j), *carry[h], None) for h in range(hp))

    def body2(j2, carry):
        out = []
        for h in range(hp):
            c = update(h, 2 * j2, scores(h, 2 * j2), *carry[h], None)
            out.append(update(h, 2 * j2 + 1, scores(h, 2 * j2 + 1), *c, None))
        return tuple(out)

    init = tuple((jnp.full((tq, 1), NEG_BIG, F32), jnp.zeros((tq, 1), F32), jnp.zeros((tq, MLA_V), F32))
                 for h in range(hp))
    n_full = i * sub
    carry = lax.fori_loop(0, n_full // 2, body2, init)
    carry = lax.fori_loop(2 * (n_full // 2), n_full, body, carry)
    for h in range(hp):
        m, l, acc = carry[h]
        for jj in range(sub):
            m, l, acc = update(h, n_full + jj, scores(h, n_full + jj), m, l, acc, jj)
        o_ref[:, h * MLA_V:(h + 1) * MLA_V] = (acc / l).astype(o_ref.dtype)


def _attention(q, k, v):
    B, H, S, _ = q.shape
    tq = _tile(S, 512)
    tk = _tile(tq, 512)
    hp = 4
    return pl.pallas_call(
        functools.partial(_attn_kernel, tq=tq, tk=tk, hp=hp),
        grid=(B, H // hp, S // tq),
        in_specs=[
            pl.BlockSpec((None, hp, tq, QK_WIDTH), lambda b, h, i: (b, h, i, 0)),
            pl.BlockSpec((None, hp, S, QK_WIDTH), lambda b, h, i: (b, h, 0, 0)),
            pl.BlockSpec((None, hp, S, MLA_V), lambda b, h, i: (b, h, 0, 0)),
        ],
        out_specs=pl.BlockSpec((None, tq, hp * MLA_V), lambda b, h, i: (b, i, h)),
        out_shape=jax.ShapeDtypeStruct((B, S, H * MLA_V), BF16),
        compiler_params=_params(("arbitrary", "arbitrary", "arbitrary")),
        name="attention",
    )(q, k, v)


def _retention_kernel(q_ref, k_ref, v_ref, g_ref, cos_ref, sin_ref, o_ref,
                      state_ref, decay_ref, qdec_ref, kdec_ref, *, tb):
    first_block = pl.program_id(1) == 0
    log_g = [float(np.log1p(-np.exp2(-5.0 - h))) for h in range(RET_HEADS)]

    @pl.when(jnp.logical_and(pl.program_id(0) == 0, first_block))
    def _():
        rel = (lax.broadcasted_iota(I32, (tb, tb), 0) - lax.broadcasted_iota(I32, (tb, tb), 1)).astype(F32)
        idx = lax.broadcasted_iota(I32, (tb, RET_DK), 0).astype(F32)
        for h in range(RET_HEADS):
            decay_ref[h] = jnp.where(rel >= 0, jnp.exp(jnp.maximum(rel, 0.0) * log_g[h]), 0.0)
            qdec_ref[h] = jnp.exp((idx + 1.0) * log_g[h])
            kdec_ref[h] = jnp.exp((tb - 1.0 - idx) * log_g[h])

    @pl.when(first_block)
    def _():
        state_ref[...] = jnp.zeros_like(state_ref)

    cosr = cos_ref[...]
    sinr = sin_ref[...]
    for h in range(RET_HEADS):
        sl = slice(h * RET_DK, (h + 1) * RET_DK)
        q = _rope_block(q_ref[:, sl].astype(F32), cosr, sinr)
        k = _rope_block(k_ref[:, sl].astype(F32), cosr, sinr) * (RET_DK ** -0.5)
        vb = v_ref[:, sl]
        scores = lax.dot_general(q.astype(BF16), k.astype(BF16), _NT, preferred_element_type=F32) * decay_ref[h]
        y = jnp.dot(scores.astype(BF16), vb, preferred_element_type=F32)
        state = state_ref[h]
        q_dec = q * qdec_ref[h]
        y = y + jnp.dot(q_dec.astype(BF16), state.astype(BF16), preferred_element_type=F32)
        k_dec = k * kdec_ref[h]
        upd = lax.dot_general(k_dec.astype(BF16), vb, _TN, preferred_element_type=F32)
        state_ref[h] = state * float(np.exp(tb * log_g[h])) + upd
        mu = jnp.mean(y, axis=-1, keepdims=True)
        yc = y - mu
        var = jnp.mean(yc * yc, axis=-1, keepdims=True)
        g = g_ref[:, sl].astype(F32)
        o_ref[:, sl] = (g * jax.nn.sigmoid(g) * (yc * lax.rsqrt(var + EPS))).astype(o_ref.dtype)


def _retention(proj, cosr, sinr):
    B, S, _ = proj.shape
    tb = _tile(S, 256)
    W = RET_HEADS * RET_DK
    col = lambda g: pl.BlockSpec((None, tb, W), lambda b, i: (b, i, g))
    return pl.pallas_call(
        functools.partial(_retention_kernel, tb=tb),
        grid=(B, S // tb),
        in_specs=[col(1), col(2), col(3), col(4),
                  pl.BlockSpec((tb, LANES), lambda b, i: (i, 0)),
                  pl.BlockSpec((tb, LANES), lambda b, i: (i, 0))],
        out_specs=pl.BlockSpec((None, tb, W), lambda b, i: (b, i, 0)),
        out_shape=jax.ShapeDtypeStruct((B, S, W), BF16),
        scratch_shapes=[pltpu.VMEM((RET_HEADS, RET_DK, RET_DV), F32),
                        pltpu.VMEM((RET_HEADS, tb, tb), F32),
                        pltpu.VMEM((RET_HEADS, tb, RET_DK), F32),
                        pltpu.VMEM((RET_HEADS, tb, RET_DK), F32)],
        compiler_params=_params(("arbitrary", "arbitrary")),
        name="retention",
    )(proj, proj, proj, proj, cosr, sinr)


def _oproj_kernel(ym_ref, yr_ref, wo_ref, x_ref, g1_ref, a2_ref, b2_ref, x1_ref, h2_ref, *, tm, ts):
    for r0 in range(0, tm, ts):
        rows = pl.ds(r0, ts)
        acc = jnp.dot(ym_ref[rows, :], wo_ref[:MIX_HALF, :], preferred_element_type=F32)
        acc = acc + jnp.dot(yr_ref[rows, :], wo_ref[MIX_HALF:, :], preferred_element_type=F32)
        x1 = x_ref[rows, :] + g1_ref[...] * acc
        x1_ref[rows, :] = x1
        h2_ref[rows, :] = (x1 * lax.rsqrt(jnp.mean(x1 * x1, axis=-1, keepdims=True) + EPS) * a2_ref[...]
                           + b2_ref[...])


def _oproj(y_mla, y_ret, wo, x, g1, a2, b2, layer):
    B, S, D = x.shape
    tm = _tile(S, 512)
    row = lambda w: pl.BlockSpec((None, tm, w), lambda b, i: (b, i, 0))
    vec = pl.BlockSpec((None, 1, D), lambda b, i: (b, 0, 0))
    return pl.pallas_call(
        functools.partial(_oproj_kernel, tm=tm, ts=_tile(tm, 256)),
        grid=(B, S // tm),
        in_specs=[row(MIX_HALF), row(MIX_HALF),
                  pl.BlockSpec((None,) + wo.shape[1:], lambda b, i: (layer, 0, 0), pipeline_mode=pl.Buffered(1)),
                  row(D), vec, vec, vec],
        out_specs=[row(D), row(D)],
        out_shape=[jax.ShapeDtypeStruct((B, S, D), F32), jax.ShapeDtypeStruct((B, S, D), F32)],
        compiler_params=_params(("arbitrary", "arbitrary")),
        name="oproj",
    )(y_mla, y_ret, wo, x, g1, a2, b2)


ROUTE_ROWS = 8


def _router_kernel(h2_ref, wr_ref, br_ref, ri_ref, rw_ref, cnt_ref, carry_ref, *, tr):
    @pl.when(pl.program_id(0) == 0)
    def _():
        carry_ref[...] = jnp.zeros_like(carry_ref)

    h2 = h2_ref[...]
    h_hi = h2.astype(BF16)
    h_lo = (h2 - h_hi.astype(F32)).astype(BF16)
    lg2 = (jnp.dot(h_hi, wr_ref[...], preferred_element_type=F32)
           + jnp.dot(h_lo, wr_ref[...], preferred_element_type=F32))
    lg = lg2[:, :LANES] + lg2[:, LANES:] + br_ref[...]

    lane = lax.broadcasted_iota(I32, (tr, LANES), 1)
    big = jnp.int32(1 << 20)
    is_g = lane < N_GROUPS
    e_lane = lane - N_GROUPS
    is_e = jnp.logical_and(e_lane >= 0, e_lane < N_EXPERTS)
    gl = jnp.where(is_g, lg, NEG_BIG)
    gmax = jnp.max(gl, axis=-1, keepdims=True)
    gsel = jnp.min(jnp.where(gl == gmax, lane, big), axis=-1, keepdims=True)
    p_group = 1.0 / jnp.sum(jnp.where(is_g, jnp.exp(gl - gmax), 0.0), axis=-1, keepdims=True)
    in_grp = jnp.logical_and(is_e, jnp.right_shift(e_lane, EPG_SHIFT) == gsel)
    el = jnp.where(in_grp, lg, NEG_BIG)
    t1 = jnp.max(el, axis=-1, keepdims=True)
    i1 = jnp.min(jnp.where(el == t1, lane, big), axis=-1, keepdims=True)
    el2 = jnp.where(lane == i1, NEG_BIG, el)
    t2 = jnp.max(el2, axis=-1, keepdims=True)
    i2 = jnp.min(jnp.where(el2 == t2, lane, big), axis=-1, keepdims=True)
    ex = jnp.exp(t2 - t1)
    w1 = p_group / (1.0 + ex)
    w2 = p_group * ex / (1.0 + ex)

    oh1 = lane == i1
    oh2 = lane == i2
    cnt = jnp.where(jnp.logical_or(oh1, oh2), 1.0, 0.0)
    strict_lower = jnp.where(lax.broadcasted_iota(I32, (tr, tr), 1) < lax.broadcasted_iota(I32, (tr, tr), 0),
                             1.0, 0.0).astype(BF16)
    before = jnp.dot(strict_lower, cnt.astype(BF16), preferred_element_type=F32) + carry_ref[...]
    r1 = jnp.sum(jnp.where(oh1, before, 0.0), axis=-1, keepdims=True)
    r2 = jnp.sum(jnp.where(oh2, before, 0.0), axis=-1, keepdims=True)
    carry_ref[...] = carry_ref[...] + jnp.sum(cnt, axis=0, keepdims=True)
    cnt_ref[...] = carry_ref[...]

    rec = jnp.where(lane == 0, (i1 - N_GROUPS).astype(F32), jnp.where(lane == 1, (i2 - N_GROUPS).astype(F32),
                    jnp.where(lane == 2, r1, jnp.where(lane == 3, r2, 0.0))))
    ri_ref[...] = jnp.transpose(rec)[:ROUTE_ROWS, :]
    rw_ref[...] = jnp.where(lane == 0, w1, jnp.where(lane == 1, w2, 0.0))


def _router(h2, wr, br, layer):
    T, D = h2.shape
    tr = _tile(T, 1024)
    return pl.pallas_call(
        functools.partial(_router_kernel, tr=tr),
        grid=(T // tr,),
        in_specs=[pl.BlockSpec((tr, D), lambda i: (i, 0)),
                  pl.BlockSpec((None,) + wr.shape[1:], lambda i: (layer, 0, 0)),
                  pl.BlockSpec((None,) + br.shape[1:], lambda i: (layer, 0, 0))],
        out_specs=[pl.BlockSpec((ROUTE_ROWS, tr), lambda i: (0, i)),
                   pl.BlockSpec((tr, LANES), lambda i: (i, 0)),
                   pl.BlockSpec((1, LANES), lambda i: (0, 0))],
        out_shape=[jax.ShapeDtypeStruct((ROUTE_ROWS, T), F32),
                   jax.ShapeDtypeStruct((T, LANES), F32),
                   jax.ShapeDtypeStruct((1, LANES), F32)],
        scratch_shapes=[pltpu.VMEM((1, LANES), F32)],
        compiler_params=_params(("arbitrary",)),
        name="router",
    )(h2, wr, br)


def _dispatch_kernel(pos_ref, zt_ref, h_ref, xs_ref, zero_ref, sem, zsem, *, tm, n_tok, te):
    base = pl.program_id(0) * tm

    @pl.when(pl.program_id(0) == 0)
    def _():
        zero_ref[...] = jnp.zeros_like(zero_ref)

        def zero_copy(n):
            row = pl.multiple_of(zt_ref[n] * te, te)
            return pltpu.make_async_copy(zero_ref, xs_ref.at[pl.ds(row, te)], zsem)

        for n in range(zt_ref.shape[0]):
            @pl.when(zt_ref[n] >= 0)
            def _():
                zero_copy(n).start()
        for n in range(zt_ref.shape[0]):
            @pl.when(zt_ref[n] >= 0)
            def _():
                zero_copy(n).wait()

    def row_copy(r, p):
        return pltpu.make_async_copy(h_ref.at[pl.ds(r, 1)], xs_ref.at[pl.ds(p, 1)], sem)

    def issue(r, c):
        row_copy(r, pos_ref[base + r]).start()
        row_copy(r, pos_ref[n_tok + base + r]).start()
        return c

    lax.fori_loop(0, tm, issue, 0, unroll=8)
    for _ in range(2):
        pltpu.make_async_copy(h_ref, xs_ref.at[pl.ds(0, tm)], sem).wait()


def _dispatch(pos, zero_tiles, h2, n_rows, te):
    T, D = h2.shape
    tm = _tile(T, 512)
    grid_spec = pltpu.PrefetchScalarGridSpec(
        num_scalar_prefetch=2,
        grid=(T // tm,),
        in_specs=[pl.BlockSpec((tm, D), lambda i, pos, zt: (i, 0))],
        out_specs=pl.BlockSpec(memory_space=pl.ANY),
        scratch_shapes=[pltpu.VMEM((te, D), F32), pltpu.SemaphoreType.DMA(()), pltpu.SemaphoreType.DMA(())],
    )
    return pl.pallas_call(
        functools.partial(_dispatch_kernel, tm=tm, n_tok=T, te=te),
        grid_spec=grid_spec,
        out_shape=jax.ShapeDtypeStruct((n_rows, D), F32),
        compiler_params=_params(("arbitrary",)),
        name="dispatch",
    )(pos, zero_tiles, h2)


def _expert_kernel(te_ref, nx_ref, nt_ref, xs_ref, wg_hbm, wu_hbm, wd_hbm, ys_ref,
                   wgf_ref, wuf_ref, wdf_ref, wgb_ref, wub_ref, wdb_ref, sem, *, layer):
    i = pl.program_id(0)
    used = i < nt_ref[0]
    expert = te_ref[i]
    new_expert = jnp.logical_or(i == 0, expert != te_ref[jnp.maximum(i - 1, 0)])

    def fetch(e):
        return (pltpu.make_async_copy(wg_hbm.at[layer, e], wgf_ref, sem.at[0]),
                pltpu.make_async_copy(wu_hbm.at[layer, e], wuf_ref, sem.at[1]),
                pltpu.make_async_copy(wd_hbm.at[layer, e], wdf_ref, sem.at[2]))

    @pl.when(i == 0)
    def _():
        for c in fetch(expert):
            c.start()

    @pl.when(jnp.logical_and(used, new_expert))
    def _():
        for c in fetch(expert):
            c.wait()
        wgb_ref[...] = wgf_ref[...].astype(BF16)
        wub_ref[...] = wuf_ref[...].astype(BF16)
        wdb_ref[...] = wdf_ref[...].astype(BF16)

        @pl.when(nx_ref[i] >= 0)
        def _():
            for c in fetch(nx_ref[i]):
                c.start()

    @pl.when(jnp.logical_not(used))
    def _():
        ys_ref[...] = jnp.zeros_like(ys_ref)

    @pl.when(used)
    def _():
        xb = xs_ref[...].astype(BF16)
        a = jnp.dot(xb, wgb_ref[...], preferred_element_type=F32)
        u = jnp.dot(xb, wub_ref[...], preferred_element_type=F32)
        hid = (a * jax.nn.sigmoid(a) * u).astype(BF16)
        ys_ref[...] = jnp.dot(hid, wdb_ref[...], preferred_element_type=F32)


def _experts(tile_expert, next_expert, n_tiles, xs, wg, wu, wd, layer, te):
    P, D = xs.shape
    F = wg.shape[-1]
    row_idx = lambda i, te_ref, nx_ref, nt_ref: (jnp.minimum(i, nt_ref[0] - 1), 0)
    hbm = pl.BlockSpec(memory_space=pl.ANY)
    grid_spec = pltpu.PrefetchScalarGridSpec(
        num_scalar_prefetch=3,
        grid=(P // te,),
        in_specs=[pl.BlockSpec((te, D), row_idx), hbm, hbm, hbm],
        out_specs=pl.BlockSpec((te, D), lambda i, te_ref, nx_ref, nt_ref: (i, 0)),
        scratch_shapes=[pltpu.VMEM((D, F), F32), pltpu.VMEM((D, F), F32), pltpu.VMEM((F, D), F32),
                        pltpu.VMEM((D, F), BF16), pltpu.VMEM((D, F), BF16), pltpu.VMEM((F, D), BF16),
                        pltpu.SemaphoreType.DMA((3,))],
    )
    return pl.pallas_call(
        functools.partial(_expert_kernel, layer=layer),
        grid_spec=grid_spec,
        out_shape=jax.ShapeDtypeStruct((P, D), F32),
        compiler_params=_params(("arbitrary",)),
        name="experts",
    )(tile_expert, next_expert, n_tiles, xs, wg, wu, wd)


def _combine_kernel(pos_ref, x1_ref, rw_ref, g2_ref, fg_ref, ys_ref, o_ref, buf_ref, sem, *, tm, n_tok, final):
    tile = pl.program_id(0) * pl.num_programs(1) + pl.program_id(1)
    n_tile = pl.num_programs(0) * pl.num_programs(1)
    slot = lax.rem(tile, 2)

    def gather(t, s):
        base = t * tm

        def issue(r, c):
            for k in range(2):
                pltpu.make_async_copy(ys_ref.at[pl.ds(pos_ref[k * n_tok + base + r], 1)],
                                      buf_ref.at[s, k, pl.ds(r, 1)], sem.at[s]).start()
            return c

        lax.fori_loop(0, tm, issue, 0, unroll=8)

    @pl.when(tile == 0)
    def _():
        gather(tile, slot)

    @pl.when(tile + 1 < n_tile)
    def _():
        gather(tile + 1, 1 - slot)

    for k in range(2):
        pltpu.make_async_copy(ys_ref.at[pl.ds(0, tm)], buf_ref.at[slot, k], sem.at[slot]).wait()
    w = rw_ref[...]
    moe = w[:, 0:1] * buf_ref[slot, 0] + w[:, 1:2] * buf_ref[slot, 1]
    x2 = x1_ref[...] + g2_ref[...] * moe
    if final:
        x2 = x2 * lax.rsqrt(jnp.mean(x2 * x2, axis=-1, keepdims=True) + EPS) * fg_ref[...]
    o_ref[...] = x2


def _combine(pos, x1, rw, g2, fg, ys, final):
    B, S, D = x1.shape
    tm = _tile(S, 512)
    grid_spec = pltpu.PrefetchScalarGridSpec(
        num_scalar_prefetch=1,
        grid=(B, S // tm),
        in_specs=[pl.BlockSpec((None, tm, D), lambda b, i, pos: (b, i, 0)),
                  pl.BlockSpec((None, tm, LANES), lambda b, i, pos: (b, i, 0)),
                  pl.BlockSpec((None, 1, D), lambda b, i, pos: (b, 0, 0)),
                  pl.BlockSpec((1, D), lambda b, i, pos: (0, 0)),
                  pl.BlockSpec(memory_space=pl.ANY)],
        out_specs=pl.BlockSpec((None, tm, D), lambda b, i, pos: (b, i, 0)),
        scratch_shapes=[pltpu.VMEM((2, 2, tm, D), F32), pltpu.SemaphoreType.DMA((2,))],
    )
    return pl.pallas_call(
        functools.partial(_combine_kernel, tm=tm, n_tok=B * S, final=final),
        grid_spec=grid_spec,
        out_shape=jax.ShapeDtypeStruct((B, S, D), F32),
        compiler_params=_params(("arbitrary", "arbitrary")),
        name="combine",
    )(pos, x1, rw, g2, fg, ys)


def _rope_tables(S, half):
    inv = ROPE_BASE ** (-jnp.arange(half, dtype=F32) / half)
    ang = jnp.arange(S, dtype=jnp.int32).astype(F32)[:, None] * inv[None, :]
    cos, sin = jnp.cos(ang), jnp.sin(ang)
    z = jnp.zeros((S, LANES // 2 - half), F32)
    return (jnp.concatenate([cos, z, cos, z], axis=-1), jnp.concatenate([-sin, z, sin, z], axis=-1))


def _layout_w_in(w_in):
    n_head = MLA_Q_RANK + MLA_KV_RANK + MLA_ROPE
    half = MLA_ROPE // 2
    z = lambda n: jnp.zeros(w_in.shape[:-1] + (n,), w_in.dtype)
    kpe = w_in[..., MLA_Q_RANK + MLA_KV_RANK:n_head]
    head = [w_in[..., :MLA_Q_RANK + MLA_KV_RANK], kpe[..., :half], z(LANES // 2 - half), kpe[..., half:],
            z(LANES // 2 - half), z(PROJ_GROUP - MLA_Q_RANK - MLA_KV_RANK - LANES)]
    return jnp.concatenate(head, axis=-1).astype(BF16), w_in[..., n_head:].astype(BF16)


def _layout_w_uq(w_uq):
    lead = w_uq.shape[:-1]
    half = MLA_ROPE // 2
    w = w_uq.reshape(lead + (MLA_HEADS, MLA_NOPE + MLA_ROPE))
    z = jnp.zeros(lead + (MLA_HEADS, LANES // 2 - half), w_uq.dtype)
    w = jnp.concatenate([w[..., :MLA_NOPE], w[..., MLA_NOPE:MLA_NOPE + half], z, w[..., MLA_NOPE + half:], z], axis=-1)
    return w.reshape(lead + (MLA_HEADS * QK_WIDTH,)).astype(BF16)


def kernel(x, c, ada_w, ada_b, norm1_g, w_in, q_norm_g, w_uq, kv_norm_g, w_ukv, w_o, norm2_g, router_group_w, router_group_b, router_expert_w, router_expert_b, w_gate, w_up, w_down, final_norm_g):
    B, S, D = x.shape
    L = ada_w.shape[0]
    T = B * S
    te = 256
    n_rows = 2 * T + N_EXPERTS * te
    n_tiles_max = n_rows // te

    mod = _adaln_mod(c, ada_w, ada_b)
    cos_m, sin_m = _rope_tables(S, MLA_ROPE // 2)
    cos_r, sin_r = _rope_tables(S, RET_DK // 2)
    fg = final_norm_g.reshape(1, D)

    w_in_head, w_in_tail = _layout_w_in(w_in)
    w_uq_k = _layout_w_uq(w_uq)
    w_ukv_k = w_ukv.astype(BF16)
    w_o_k = w_o.astype(BF16)
    qg_k = q_norm_g[:, None, :]
    kvg_k = kv_norm_g[:, None, :]
    wr = jnp.concatenate([router_group_w, router_expert_w,
                          jnp.zeros((L, D, LANES - N_GROUPS - N_EXPERTS), F32)], axis=-1)
    wr_hi = wr.astype(BF16)
    wr_k = jnp.concatenate([wr_hi, (wr - wr_hi.astype(F32)).astype(BF16)], axis=-1)
    br_k = jnp.concatenate([router_group_b, router_expert_b,
                            jnp.zeros((L, LANES - N_GROUPS - N_EXPERTS), F32)], axis=-1)[:, None, :]

    for l in range(L):
        sh1, sc1, g1, sh2, sc2, g2 = [mod[l, :, i * D:(i + 1) * D][:, None, :] for i in range(6)]
        a1 = norm1_g[l][None, None, :] * (1.0 + sc1)
        a2 = norm2_g[l][None, None, :] * (1.0 + sc2)

        proj = _norm_matmul(x, a1, sh1, w_in_head, w_in_tail, l)
        q, k, v = _mla_prep(proj, qg_k, kvg_k, w_uq_k, w_ukv_k, cos_m, sin_m, l)
        y_mla = _attention(q, k, v)
        y_ret = _retention(proj, cos_r, sin_r)

        x1, h2 = _oproj(y_mla, y_ret, w_o_k, x, g1, a2, sh2, l)
        h2 = h2.reshape(T, D)
        ri, rw, cnt = _router(h2, wr_k, br_k, l)
        rw = rw.reshape(B, S, LANES)

        counts = cnt[0, N_GROUPS:N_GROUPS + N_EXPERTS].astype(I32)
        padded = ((counts + te - 1) // te) * te
        ends = jnp.cumsum(padded)
        ri = ri.astype(I32)
        eids = jnp.arange(N_EXPERTS, dtype=I32)
        experts_of = jnp.concatenate([ri[0], ri[1]])
        ranks_of = jnp.concatenate([ri[2], ri[3]])
        pos = ranks_of + jnp.sum(jnp.where(eids[:, None] < experts_of[None, :], padded[:, None], 0), axis=0)
        n_tiles = (ends[-1] // te).astype(I32)
        tile_ids = jnp.minimum(jnp.arange(n_tiles_max, dtype=I32), n_tiles - 1)
        tile_expert = jnp.sum((ends // te)[None, :] <= tile_ids[:, None], axis=1).astype(I32)
        spare = n_tiles + jnp.arange(n_tiles_max - 2 * T // te, dtype=I32)
        zero_tiles = jnp.concatenate([jnp.where(padded > counts, ends // te - 1, -1),
                                      jnp.where(spare < n_tiles_max, spare, -1)]).astype(I32)

        xs = _dispatch(pos, zero_tiles, h2, n_rows, te)
        later = jnp.logical_and(eids[None, :] > eids[:, None], (padded > 0)[None, :])
        next_of = jnp.min(jnp.where(later, eids[None, :], N_EXPERTS), axis=1)
        next_of = jnp.where(next_of < N_EXPERTS, next_of, -1)
        next_expert = jnp.sum(jnp.where(tile_expert[:, None] == eids[None, :], next_of[None, :], 0), axis=1).astype(I32)
        ys = _experts(tile_expert, next_expert, n_tiles.reshape(1), xs, w_gate, w_up, w_down, l, te)
        x = _combine(pos, x1, rw, g2, fg, ys, final=(l == L - 1))
    return x
```

```python
import functools

import numpy as np
import jax
import jax.numpy as jnp
from jax import lax
from jax.experimental import pallas as pl
from jax.experimental.pallas import tpu as pltpu

F32 = jnp.float32
BF16 = jnp.bfloat16
I32 = jnp.int32
EPS = 1e-6
ROPE_BASE = 10000.0
CHUNK = 64

MLA_HEADS = 8
MLA_Q_RANK = 512
MLA_KV_RANK = 256
MLA_NOPE = 128
MLA_ROPE = 64
MLA_V = 128
RET_HEADS = 8
RET_DK = 128
RET_DV = 128
N_GROUPS = 4
EXPERTS_PER_GROUP = 8
N_EXPERTS = N_GROUPS * EXPERTS_PER_GROUP
CHUNK_SHIFT = CHUNK.bit_length() - 1
EPG_SHIFT = EXPERTS_PER_GROUP.bit_length() - 1

LANES = 128
QK_WIDTH = 2 * LANES
MIX_HALF = MLA_HEADS * MLA_V
PROJ_GROUP = 1024
PROJ_WIDTH = 5 * PROJ_GROUP
NEG_BIG = -1e30
VMEM_LIMIT = 56 * 1024 * 1024


def _tile(n, pref):
    if n <= pref:
        return n
    t = pref - pref % LANES
    while n % t:
        t -= LANES
    assert t > 0, (n, pref)
    return t


def _params(sem, vmem=VMEM_LIMIT):
    return pltpu.CompilerParams(dimension_semantics=sem, vmem_limit_bytes=vmem)


def _mod_kernel(c_ref, w_ref, b_ref, o_ref):
    c = c_ref[...]
    s = (c * jax.nn.sigmoid(c)).astype(BF16)
    o_ref[...] = jnp.dot(s, w_ref[...].astype(BF16), preferred_element_type=F32) + b_ref[...]


def _adaln_mod(c, ada_w, ada_b):
    B, D = c.shape
    L, _, N = ada_w.shape
    bp = 8
    cp = jnp.pad(c, ((0, bp - B), (0, 0)))
    tn = _tile(N, 1024)
    out = pl.pallas_call(
        _mod_kernel,
        grid=(L, N // tn),
        in_specs=[
            pl.BlockSpec((bp, D), lambda l, j: (0, 0)),
            pl.BlockSpec((None, D, tn), lambda l, j: (l, 0, j)),
            pl.BlockSpec((None, 1, tn), lambda l, j: (l, 0, j)),
        ],
        out_specs=pl.BlockSpec((None, bp, tn), lambda l, j: (l, 0, j)),
        out_shape=jax.ShapeDtypeStruct((L, bp, N), F32),
        compiler_params=_params(("arbitrary", "arbitrary")),
        name="adaln_mod",
    )(cp, ada_w, ada_b.reshape(L, 1, N))
    return out[:, :B]


def _norm_matmul_kernel(x_ref, a_ref, b_ref, wa_ref, wb_ref, o_ref, h_ref):
    j = pl.program_id(2)

    @pl.when(j == 0)
    def _():
        x = x_ref[...]
        ms = jnp.mean(x * x, axis=-1, keepdims=True)
        h_ref[...] = (x * lax.rsqrt(ms + EPS) * a_ref[...] + b_ref[...]).astype(BF16)
        o_ref[...] = jnp.dot(h_ref[...], wa_ref[...], preferred_element_type=F32).astype(o_ref.dtype)

    @pl.when(j > 0)
    def _():
        o_ref[...] = jnp.dot(h_ref[...], wb_ref[...], preferred_element_type=F32).astype(o_ref.dtype)


def _norm_matmul(x, a, b, wa, wb, layer):
    B, S, D = x.shape
    tn = PROJ_GROUP
    nb = wb.shape[-1] // tn
    tm = _tile(S, 1024)
    return pl.pallas_call(
        _norm_matmul_kernel,
        grid=(B, S // tm, 1 + nb),
        in_specs=[
            pl.BlockSpec((None, tm, D), lambda b, i, j: (b, i, 0)),
            pl.BlockSpec((None, 1, D), lambda b, i, j: (b, 0, 0)),
            pl.BlockSpec((None, 1, D), lambda b, i, j: (b, 0, 0)),
            pl.BlockSpec((None, D, tn), lambda b, i, j: (layer, 0, 0)),
            pl.BlockSpec((None, D, tn), lambda b, i, j: (layer, 0, jnp.maximum(j - 1, 0))),
        ],
        out_specs=pl.BlockSpec((None, tm, tn), lambda b, i, j: (b, i, j)),
        out_shape=jax.ShapeDtypeStruct((B, S, (1 + nb) * tn), BF16),
        scratch_shapes=[pltpu.VMEM((tm, D), BF16)],
        compiler_params=_params(("arbitrary", "arbitrary", "arbitrary")),
        name="norm_matmul",
    )(x, a, b, wa, wb)


def _rms(x, g):
    return x * lax.rsqrt(jnp.mean(x * x, axis=-1, keepdims=True) + EPS) * g


def _rope_block(x, cosb, sinb):
    return x * cosb + pltpu.roll(x, LANES // 2, 1) * sinb


def _mla_prep_kernel(p_ref, qg_ref, kvg_ref, wuq_ref, wukv_ref, cos_ref, sin_ref, q_ref, k_ref, v_ref):
    p = p_ref[...].astype(F32)
    cq = p[:, :MLA_Q_RANK]
    ckv = p[:, MLA_Q_RANK:MLA_Q_RANK + MLA_KV_RANK]
    kpe = p[:, MLA_Q_RANK + MLA_KV_RANK:MLA_Q_RANK + MLA_KV_RANK + LANES]
    cosb = cos_ref[...]
    sinb = sin_ref[...]
    scale = (MLA_NOPE + MLA_ROPE) ** -0.5 * float(np.log2(np.e))
    qf = jnp.dot(_rms(cq, qg_ref[...]).astype(BF16), wuq_ref[...], preferred_element_type=F32)
    kvf = jnp.dot(_rms(ckv, kvg_ref[...]).astype(BF16), wukv_ref[...], preferred_element_type=F32)
    kpe_r = _rope_block(kpe, cosb, sinb).astype(BF16)
    for h in range(MLA_HEADS):
        c0 = h * QK_WIDTH
        q_ref[h, :, :LANES] = (qf[:, c0:c0 + LANES] * scale).astype(BF16)
        q_ref[h, :, LANES:] = (_rope_block(qf[:, c0 + LANES:c0 + QK_WIDTH], cosb, sinb) * scale).astype(BF16)
        k_ref[h, :, :LANES] = kvf[:, c0:c0 + LANES].astype(BF16)
        k_ref[h, :, LANES:] = kpe_r
        v_ref[h] = kvf[:, c0 + LANES:c0 + QK_WIDTH].astype(BF16)


def _mla_prep(proj, qg, kvg, wuq, wukv, cosb, sinb, layer):
    B, S, _ = proj.shape
    H = MLA_HEADS
    tm = _tile(S, 512)
    return pl.pallas_call(
        _mla_prep_kernel,
        grid=(B, S // tm),
        in_specs=[
            pl.BlockSpec((None, tm, PROJ_GROUP), lambda b, i: (b, i, 0)),
            pl.BlockSpec((None, 1, MLA_Q_RANK), lambda b, i: (layer, 0, 0)),
            pl.BlockSpec((None, 1, MLA_KV_RANK), lambda b, i: (layer, 0, 0)),
            pl.BlockSpec((None, MLA_Q_RANK, H * QK_WIDTH), lambda b, i: (layer, 0, 0)),
            pl.BlockSpec((None, MLA_KV_RANK, H * QK_WIDTH), lambda b, i: (layer, 0, 0)),
            pl.BlockSpec((tm, LANES), lambda b, i: (i, 0)),
            pl.BlockSpec((tm, LANES), lambda b, i: (i, 0)),
        ],
        out_specs=[
            pl.BlockSpec((None, H, tm, QK_WIDTH), lambda b, i: (b, 0, i, 0)),
            pl.BlockSpec((None, H, tm, QK_WIDTH), lambda b, i: (b, 0, i, 0)),
            pl.BlockSpec((None, H, tm, MLA_V), lambda b, i: (b, 0, i, 0)),
        ],
        out_shape=[
            jax.ShapeDtypeStruct((B, H, S, QK_WIDTH), BF16),
            jax.ShapeDtypeStruct((B, H, S, QK_WIDTH), BF16),
            jax.ShapeDtypeStruct((B, H, S, MLA_V), BF16),
        ],
        compiler_params=_params(("arbitrary", "arbitrary")),
        name="mla_prep",
    )(proj, qg, kvg, wuq, wukv, cosb, sinb)


_NT = (((1,), (1,)), ((), ()))
_TN = (((0,), (0,)), ((), ()))


def _attn_kernel(q_ref, k_ref, v_ref, o_ref, *, tq, tk, hp):
    i = pl.program_id(2)
    sub = tq // tk

    def scores(h, j):
        start = pl.multiple_of(j * tk, tk)
        return lax.dot_general(q_ref[h], k_ref[h, pl.ds(start, tk), :], _NT, preferred_element_type=F32)

    def update(h, j, s, m, l, acc, masked_sub):
        if masked_sub is not None:
            qc = jnp.right_shift(lax.broadcasted_iota(I32, (tq, tk), 0), CHUNK_SHIFT)
            kc = jnp.right_shift(lax.broadcasted_iota(I32, (tq, tk), 1) + masked_sub * tk, CHUNK_SHIFT)
            s = jnp.where(kc <= qc, s, NEG_BIG)
        m_new = jnp.maximum(m, jnp.max(s, axis=-1, keepdims=True))
        alpha = jnp.exp2(m - m_new)
        p = jnp.exp2(s - m_new)
        l = alpha * l + jnp.sum(p, axis=-1, keepdims=True)
        start = pl.multiple_of(j * tk, tk)
        acc = alpha * acc + jnp.dot(p.astype(BF16), v_ref[h, pl.ds(start, tk), :], preferred_element_type=F32)
        return m_new, l, acc

    def body(j, carry):
        return tuple(update(h, j, scores(h, j), *carry[h], None) for h in range(hp))

    def body2(j2, carry):
        out = []
        for h in range(hp):
            c = update(h, 2 * j2, scores(h, 2 * j2), *carry[h], None)
            out.append(update(h, 2 * j2 + 1, scores(h, 2 * j2 + 1), *c, None))
        return tuple(out)

    init = tuple((jnp.full((tq, 1), NEG_BIG, F32), jnp.zeros((tq, 1), F32), jnp.zeros((tq, MLA_V), F32))
                 for h in range(hp))
    n_full = i * sub
    carry = lax.fori_loop(0, n_full // 2, body2, init)
    carry = lax.fori_loop(2 * (n_full // 2), n_full, body, carry)
    for h in range(hp):
        m, l, acc = carry[h]
        for jj in range(sub):
            m, l, acc = update(h, n_full + jj, scores(h, n_full + jj), m, l, acc, jj)
        o_ref[:, h * MLA_V:(h + 1) * MLA_V] = (acc / l).astype(o_ref.dtype)


def _attention(q, k, v):
    B, H, S, _ = q.shape
    tq = _tile(S, 512)
    tk = _tile(tq, 512)
    hp = 4
    return pl.pallas_call(
        functools.partial(_attn_kernel, tq=tq, tk=tk, hp=hp),
        grid=(B, H // hp, S // tq),
        in_specs=[
            pl.BlockSpec((None, hp, tq, QK_WIDTH), lambda b, h, i: (b, h, i, 0)),
            pl.BlockSpec((None, hp, S, QK_WIDTH), lambda b, h, i: (b, h, 0, 0)),
            pl.BlockSpec((None, hp, S, MLA_V), lambda b, h, i: (b, h, 0, 0)),
        ],
        out_specs=pl.BlockSpec((None, tq, hp * MLA_V), lambda b, h, i: (b, i, h)),
        out_shape=jax.ShapeDtypeStruct((B, S, H * MLA_V), BF16),
        compiler_params=_params(("arbitrary", "arbitrary", "arbitrary")),
        name="attention",
    )(q, k, v)


def _retention_kernel(q_ref, k_ref, v_ref, g_ref, cos_ref, sin_ref, o_ref,
                      state_ref, decay_ref, qdec_ref, kdec_ref, *, tb):
    first_block = pl.program_id(1) == 0
    log_g = [float(np.log1p(-np.exp2(-5.0 - h))) for h in range(RET_HEADS)]

    @pl.when(jnp.logical_and(pl.program_id(0) == 0, first_block))
    def _():
        rel = (lax.broadcasted_iota(I32, (tb, tb), 0) - lax.broadcasted_iota(I32, (tb, tb), 1)).astype(F32)
        idx = lax.broadcasted_iota(I32, (tb, RET_DK), 0).astype(F32)
        for h in range(RET_HEADS):
            decay_ref[h] = jnp.where(rel >= 0, jnp.exp(jnp.maximum(rel, 0.0) * log_g[h]), 0.0)
            qdec_ref[h] = jnp.exp((idx + 1.0) * log_g[h])
            kdec_ref[h] = jnp.exp((tb - 1.0 - idx) * log_g[h])

    @pl.when(first_block)
    def _():
        state_ref[...] = jnp.zeros_like(state_ref)

    cosr = cos_ref[...]
    sinr = sin_ref[...]
    for h in range(RET_HEADS):
        sl = slice(h * RET_DK, (h + 1) * RET_DK)
        q = _rope_block(q_ref[:, sl].astype(F32), cosr, sinr)
        k = _rope_block(k_ref[:, sl].astype(F32), cosr, sinr) * (RET_DK ** -0.5)
        vb = v_ref[:, sl]
        scores = lax.dot_general(q.astype(BF16), k.astype(BF16), _NT, preferred_element_type=F32) * decay_ref[h]
        y = jnp.dot(scores.astype(BF16), vb, preferred_element_type=F32)
        state = state_ref[h]
        q_dec = q * qdec_ref[h]
        y = y + jnp.dot(q_dec.astype(BF16), state.astype(BF16), preferred_element_type=F32)
        k_dec = k * kdec_ref[h]
        upd = lax.dot_general(k_dec.astype(BF16), vb, _TN, preferred_element_type=F32)
        state_ref[h] = state * float(np.exp(tb * log_g[h])) + upd
        mu = jnp.mean(y, axis=-1, keepdims=True)
        yc = y - mu
        var = jnp.mean(yc * yc, axis=-1, keepdims=True)
        g = g_ref[:, sl].astype(F32)
        o_ref[:, sl] = (g * jax.nn.sigmoid(g) * (yc * lax.rsqrt(var + EPS))).astype(o_ref.dtype)


def _retention(proj, cosr, sinr):
    B, S, _ = proj.shape
    tb = _tile(S, 256)
    W = RET_HEADS * RET_DK
    col = lambda g: pl.BlockSpec((None, tb, W), lambda b, i: (b, i, g))
    return pl.pallas_call(
        functools.partial(_retention_kernel, tb=tb),
        grid=(B, S // tb),
        in_specs=[col(1), col(2), col(3), col(4),
                  pl.BlockSpec((tb, LANES), lambda b, i: (i, 0)),
                  pl.BlockSpec((tb, LANES), lambda b, i: (i, 0))],
        out_specs=pl.BlockSpec((None, tb, W), lambda b, i: (b, i, 0)),
        out_shape=jax.ShapeDtypeStruct((B, S, W), BF16),
        scratch_shapes=[pltpu.VMEM((RET_HEADS, RET_DK, RET_DV), F32),
                        pltpu.VMEM((RET_HEADS, tb, tb), F32),
                        pltpu.VMEM((RET_HEADS, tb, RET_DK), F32),
                        pltpu.VMEM((RET_HEADS, tb, RET_DK), F32)],
        compiler_params=_params(("arbitrary", "arbitrary")),
        name="retention",
    )(proj, proj, proj, proj, cosr, sinr)


def _oproj_kernel(ym_ref, yr_ref, wo_ref, x_ref, g1_ref, a2_ref, b2_ref, x1_ref, h2_ref, *, tm, ts):
    for r0 in range(0, tm, ts):
        rows = pl.ds(r0, ts)
        acc = jnp.dot(ym_ref[rows, :], wo_ref[:MIX_HALF, :], preferred_element_type=F32)
        acc = acc + jnp.dot(yr_ref[rows, :], wo_ref[MIX_HALF:, :], preferred_element_type=F32)
        x1 = x_ref[rows, :] + g1_ref[...] * acc
        x1_ref[rows, :] = x1
        h2_ref[rows, :] = (x1 * lax.rsqrt(jnp.mean(x1 * x1, axis=-1, keepdims=True) + EPS) * a2_ref[...]
                           + b2_ref[...])


def _oproj(y_mla, y_ret, wo, x, g1, a2, b2, layer):
    B, S, D = x.shape
    tm = _tile(S, 512)
    row = lambda w: pl.BlockSpec((None, tm, w), lambda b, i: (b, i, 0))
    vec = pl.BlockSpec((None, 1, D), lambda b, i: (b, 0, 0))
    return pl.pallas_call(
        functools.partial(_oproj_kernel, tm=tm, ts=_tile(tm, 256)),
        grid=(B, S // tm),
        in_specs=[row(MIX_HALF), row(MIX_HALF),
                  pl.BlockSpec((None,) + wo.shape[1:], lambda b, i: (layer, 0, 0), pipeline_mode=pl.Buffered(1)),
                  row(D), vec, vec, vec],
        out_specs=[row(D), row(D)],
        out_shape=[jax.ShapeDtypeStruct((B, S, D), F32), jax.ShapeDtypeStruct((B, S, D), F32)],
        compiler_params=_params(("arbitrary", "arbitrary")),
        name="oproj",
    )(y_mla, y_ret, wo, x, g1, a2, b2)


ROUTE_ROWS = 8


def _router_kernel(h2_ref, wr_ref, br_ref, ri_ref, rw_ref, cnt_ref, carry_ref, *, tr):
    @pl.when(pl.program_id(0) == 0)
    def _():
        carry_ref[...] = jnp.zeros_like(carry_ref)

    h2 = h2_ref[...]
    h_hi = h2.astype(BF16)
    h_lo = (h2 - h_hi.astype(F32)).astype(BF16)
    lg2 = (jnp.dot(h_hi, wr_ref[...], preferred_element_type=F32)
           + jnp.dot(h_lo, wr_ref[...], preferred_element_type=F32))
    lg = lg2[:, :LANES] + lg2[:, LANES:] + br_ref[...]

    lane = lax.broadcasted_iota(I32, (tr, LANES), 1)
    big = jnp.int32(1 << 20)
    is_g = lane < N_GROUPS
    e_lane = lane - N_GROUPS
    is_e = jnp.logical_and(e_lane >= 0, e_lane < N_EXPERTS)
    gl = jnp.where(is_g, lg, NEG_BIG)
    gmax = jnp.max(gl, axis=-1, keepdims=True)
    gsel = jnp.min(jnp.where(gl == gmax, lane, big), axis=-1, keepdims=True)
    p_group = 1.0 / jnp.sum(jnp.where(is_g, jnp.exp(gl - gmax), 0.0), axis=-1, keepdims=True)
    in_grp = jnp.logical_and(is_e, jnp.right_shift(e_lane, EPG_SHIFT) == gsel)
    el = jnp.where(in_grp, lg, NEG_BIG)
    t1 = jnp.max(el, axis=-1, keepdims=True)
    i1 = jnp.min(jnp.where(el == t1, lane, big), axis=-1, keepdims=True)
    el2 = jnp.where(lane == i1, NEG_BIG, el)
    t2 = jnp.max(el2, axis=-1, keepdims=True)
    i2 = jnp.min(jnp.where(el2 == t2, lane, big), axis=-1, keepdims=True)
    ex = jnp.exp(t2 - t1)
    w1 = p_group / (1.0 + ex)
    w2 = p_group * ex / (1.0 + ex)

    oh1 = lane == i1
    oh2 = lane == i2
    cnt = jnp.where(jnp.logical_or(oh1, oh2), 1.0, 0.0)
    strict_lower = jnp.where(lax.broadcasted_iota(I32, (tr, tr), 1) < lax.broadcasted_iota(I32, (tr, tr), 0),
                             1.0, 0.0).astype(BF16)
    before = jnp.dot(strict_lower, cnt.astype(BF16), preferred_element_type=F32) + carry_ref[...]
    r1 = jnp.sum(jnp.where(oh1, before, 0.0), axis=-1, keepdims=True)
    r2 = jnp.sum(jnp.where(oh2, before, 0.0), axis=-1, keepdims=True)
    carry_ref[...] = carry_ref[...] + jnp.sum(cnt, axis=0, keepdims=True)
    cnt_ref[...] = carry_ref[...]

    rec = jnp.where(lane == 0, (i1 - N_GROUPS).astype(F32), jnp.where(lane == 1, (i2 - N_GROUPS).astype(F32),
                    jnp.where(lane == 2, r1, jnp.where(lane == 3, r2, 0.0))))
    ri_ref[...] = jnp.transpose(rec)[:ROUTE_ROWS, :]
    rw_ref[...] = jnp.where(lane == 0, w1, jnp.where(lane == 1, w2, 0.0))


def _router(h2, wr, br, layer):
    T, D = h2.shape
    tr = _tile(T, 1024)
    return pl.pallas_call(
        functools.partial(_router_kernel, tr=tr),
        grid=(T // tr,),
        in_specs=[pl.BlockSpec((tr, D), lambda i: (i, 0)),
                  pl.BlockSpec((None,) + wr.shape[1:], lambda i: (layer, 0, 0)),
                  pl.BlockSpec((None,) + br.shape[1:], lambda i: (layer, 0, 0))],
        out_specs=[pl.BlockSpec((ROUTE_ROWS, tr), lambda i: (0, i)),
                   pl.BlockSpec((tr, LANES), lambda i: (i, 0)),
                   pl.BlockSpec((1, LANES), lambda i: (0, 0))],
        out_shape=[jax.ShapeDtypeStruct((ROUTE_ROWS, T), F32),
                   jax.ShapeDtypeStruct((T, LANES), F32),
                   jax.ShapeDtypeStruct((1, LANES), F32)],
        scratch_shapes=[pltpu.VMEM((1, LANES), F32)],
        compiler_params=_params(("arbitrary",)),
        name="router",
    )(h2, wr, br)


def _invert_kernel(pos_ref, zero_hbm, src_ref, sem, *, n_tok):
    fill = pltpu.make_async_copy(zero_hbm, src_ref, sem)
    fill.start()
    fill.wait()

    def put(choice):
        def body(t, c):
            src_ref[pos_ref[choice * n_tok + t]] = t
            return c
        return body

    for choice in range(2):
        lax.fori_loop(0, n_tok, put(choice), 0, unroll=8)


def _invert(pos, n_rows):
    n_tok = pos.shape[0] // 2
    smem = pl.BlockSpec(memory_space=pltpu.SMEM)
    return pl.pallas_call(
        functools.partial(_invert_kernel, n_tok=n_tok),
        in_specs=[smem, pl.BlockSpec(memory_space=pl.ANY)],
        out_specs=smem,
        out_shape=jax.ShapeDtypeStruct((n_rows,), I32),
        scratch_shapes=[pltpu.SemaphoreType.DMA(())],
        name="invert",
    )(pos, jnp.zeros((n_rows,), I32))


def _expert_kernel(te_ref, nx_ref, nt_ref, src_ref, h_hbm, wg_hbm, wu_hbm, wd_hbm, ys_ref,
                   xa_ref, xb_ref, wgf_ref, wuf_ref, wdf_ref, wgb_ref, wub_ref, wdb_ref, sem, gsem, *, layer, te):
    i = pl.program_id(0)
    n_used = nt_ref[0]
    used = i < n_used
    slot = lax.rem(i, 2)
    xbuf = (xa_ref, xb_ref)
    expert = te_ref[i]
    new_expert = jnp.logical_or(i == 0, expert != te_ref[jnp.maximum(i - 1, 0)])

    def fetch(e):
        return (pltpu.make_async_copy(wg_hbm.at[layer, e], wgf_ref, sem.at[0]),
                pltpu.make_async_copy(wu_hbm.at[layer, e], wuf_ref, sem.at[1]),
                pltpu.make_async_copy(wd_hbm.at[layer, e], wdf_ref, sem.at[2]))

    def gather(tile, s):
        base = tile * te
        for r in range(te):
            pltpu.make_async_copy(h_hbm.at[pl.ds(src_ref[base + r], 1)], xbuf[s].at[pl.ds(r, 1)],
                                  gsem.at[s]).start()

    def gather_wait(s):
        pltpu.make_async_copy(h_hbm.at[pl.ds(0, te)], xbuf[s], gsem.at[s]).wait()

    @pl.when(i == 0)
    def _():
        for c in fetch(expert):
            c.start()
        gather(0, 0)

    @pl.when(jnp.logical_and(used, new_expert))
    def _():
        for c in fetch(expert):
            c.wait()
        wgb_ref[...] = wgf_ref[...].astype(BF16)
        wub_ref[...] = wuf_ref[...].astype(BF16)
        wdb_ref[...] = wdf_ref[...].astype(BF16)

        @pl.when(nx_ref[i] >= 0)
        def _():
            for c in fetch(nx_ref[i]):
                c.start()

    @pl.when(jnp.logical_not(used))
    def _():
        ys_ref[...] = jnp.zeros_like(ys_ref)

    for s in range(2):
        @pl.when(jnp.logical_and(used, slot == s))
        def _():
            gather_wait(s)
            gather(jnp.minimum(i + 1, n_used - 1), 1 - s)
            xb = xbuf[s][...].astype(BF16)
            a = jnp.dot(xb, wgb_ref[...], preferred_element_type=F32)
            u = jnp.dot(xb, wub_ref[...], preferred_element_type=F32)
            hid = (a * jax.nn.sigmoid(a) * u).astype(BF16)
            ys_ref[...] = jnp.dot(hid, wdb_ref[...], preferred_element_type=F32)

        @pl.when(jnp.logical_and(i == n_used - 1, slot == s))
        def _():
            gather_wait(1 - s)


def _experts(tile_expert, next_expert, n_tiles, src, h2, wg, wu, wd, layer, te):
    P = src.shape[0]
    D = h2.shape[-1]
    F = wg.shape[-1]
    hbm = pl.BlockSpec(memory_space=pl.ANY)
    grid_spec = pltpu.PrefetchScalarGridSpec(
        num_scalar_prefetch=4,
        grid=(P // te,),
        in_specs=[hbm, hbm, hbm, hbm],
        out_specs=pl.BlockSpec((te, D), lambda i, te_ref, nx_ref, nt_ref, src_ref: (i, 0)),
        scratch_shapes=[pltpu.VMEM((te, D), F32), pltpu.VMEM((te, D), F32),
                        pltpu.VMEM((D, F), F32), pltpu.VMEM((D, F), F32), pltpu.VMEM((F, D), F32),
                        pltpu.VMEM((D, F), BF16), pltpu.VMEM((D, F), BF16), pltpu.VMEM((F, D), BF16),
                        pltpu.SemaphoreType.DMA((3,)), pltpu.SemaphoreType.DMA((2,))],
    )
    return pl.pallas_call(
        functools.partial(_expert_kernel, layer=layer, te=te),
        grid_spec=grid_spec,
        out_shape=jax.ShapeDtypeStruct((P, D), F32),
        compiler_params=_params(("arbitrary",)),
        name="experts",
    )(tile_expert, next_expert, n_tiles, src, h2, wg, wu, wd)


def _combine_kernel(pos_ref, x1_ref, rw_ref, g2_ref, fg_ref, ys_ref, o_ref, buf_ref, sem, *, tm, n_tok, final):
    tile = pl.program_id(0) * pl.num_programs(1) + pl.program_id(1)
    n_tile = pl.num_programs(0) * pl.num_programs(1)
    slot = lax.rem(tile, 2)

    def gather(t, s):
        base = t * tm

        def issue(r, c):
            for k in range(2):
                pltpu.make_async_copy(ys_ref.at[pl.ds(pos_ref[k * n_tok + base + r], 1)],
                                      buf_ref.at[s, k, pl.ds(r, 1)], sem.at[s]).start()
            return c

        lax.fori_loop(0, tm, issue, 0, unroll=8)

    @pl.when(tile == 0)
    def _():
        gather(tile, slot)

    @pl.when(tile + 1 < n_tile)
    def _():
        gather(tile + 1, 1 - slot)

    for k in range(2):
        pltpu.make_async_copy(ys_ref.at[pl.ds(0, tm)], buf_ref.at[slot, k], sem.at[slot]).wait()
    w = rw_ref[...]
    moe = w[:, 0:1] * buf_ref[slot, 0] + w[:, 1:2] * buf_ref[slot, 1]
    x2 = x1_ref[...] + g2_ref[...] * moe
    if final:
        x2 = x2 * lax.rsqrt(jnp.mean(x2 * x2, axis=-1, keepdims=True) + EPS) * fg_ref[...]
    o_ref[...] = x2


def _combine(pos, x1, rw, g2, fg, ys, final):
    B, S, D = x1.shape
    tm = _tile(S, 512)
    grid_spec = pltpu.PrefetchScalarGridSpec(
        num_scalar_prefetch=1,
        grid=(B, S // tm),
        in_specs=[pl.BlockSpec((None, tm, D), lambda b, i, pos: (b, i, 0)),
                  pl.BlockSpec((None, tm, LANES), lambda b, i, pos: (b, i, 0)),
                  pl.BlockSpec((None, 1, D), lambda b, i, pos: (b, 0, 0)),
                  pl.BlockSpec((1, D), lambda b, i, pos: (0, 0)),
                  pl.BlockSpec(memory_space=pl.ANY)],
        out_specs=pl.BlockSpec((None, tm, D), lambda b, i, pos: (b, i, 0)),
        scratch_shapes=[pltpu.VMEM((2, 2, tm, D), F32), pltpu.SemaphoreType.DMA((2,))],
    )
    return pl.pallas_call(
        functools.partial(_combine_kernel, tm=tm, n_tok=B * S, final=final),
        grid_spec=grid_spec,
        out_shape=jax.ShapeDtypeStruct((B, S, D), F32),
        compiler_params=_params(("arbitrary", "arbitrary")),
        name="combine",
    )(pos, x1, rw, g2, fg, ys)


def _rope_tables(S, half):
    inv = ROPE_BASE ** (-jnp.arange(half, dtype=F32) / half)
    ang = jnp.arange(S, dtype=jnp.int32).astype(F32)[:, None] * inv[None, :]
    cos, sin = jnp.cos(ang), jnp.sin(ang)
    z = jnp.zeros((S, LANES // 2 - half), F32)
    return (jnp.concatenate([cos, z, cos, z], axis=-1), jnp.concatenate([-sin, z, sin, z], axis=-1))


def _layout_w_in(w_in):
    n_head = MLA_Q_RANK + MLA_KV_RANK + MLA_ROPE
    half = MLA_ROPE // 2
    z = lambda n: jnp.zeros(w_in.shape[:-1] + (n,), w_in.dtype)
    kpe = w_in[..., MLA_Q_RANK + MLA_KV_RANK:n_head]
    head = [w_in[..., :MLA_Q_RANK + MLA_KV_RANK], kpe[..., :half], z(LANES // 2 - half), kpe[..., half:],
            z(LANES // 2 - half), z(PROJ_GROUP - MLA_Q_RANK - MLA_KV_RANK - LANES)]
    return jnp.concatenate(head, axis=-1).astype(BF16), w_in[..., n_head:].astype(BF16)


def _layout_w_uq(w_uq):
    lead = w_uq.shape[:-1]
    half = MLA_ROPE // 2
    w = w_uq.reshape(lead + (MLA_HEADS, MLA_NOPE + MLA_ROPE))
    z = jnp.zeros(lead + (MLA_HEADS, LANES // 2 - half), w_uq.dtype)
    w = jnp.concatenate([w[..., :MLA_NOPE], w[..., MLA_NOPE:MLA_NOPE + half], z, w[..., MLA_NOPE + half:], z], axis=-1)
    return w.reshape(lead + (MLA_HEADS * QK_WIDTH,)).astype(BF16)


def kernel(x, c, ada_w, ada_b, norm1_g, w_in, q_norm_g, w_uq, kv_norm_g, w_ukv, w_o, norm2_g, router_group_w, router_group_b, router_expert_w, router_expert_b, w_gate, w_up, w_down, final_norm_g):
    B, S, D = x.shape
    L = ada_w.shape[0]
    T = B * S
    te = 256
    n_rows = 2 * T + N_EXPERTS * te
    n_tiles_max = n_rows // te

    mod = _adaln_mod(c, ada_w, ada_b)
    cos_m, sin_m = _rope_tables(S, MLA_ROPE // 2)
    cos_r, sin_r = _rope_tables(S, RET_DK // 2)
    fg = final_norm_g.reshape(1, D)

    w_in_head, w_in_tail = _layout_w_in(w_in)
    w_uq_k = _layout_w_uq(w_uq)
    w_ukv_k = w_ukv.astype(BF16)
    w_o_k = w_o.astype(BF16)
    qg_k = q_norm_g[:, None, :]
    kvg_k = kv_norm_g[:, None, :]
    wr = jnp.concatenate([router_group_w, router_expert_w,
                          jnp.zeros((L, D, LANES - N_GROUPS - N_EXPERTS), F32)], axis=-1)
    wr_hi = wr.astype(BF16)
    wr_k = jnp.concatenate([wr_hi, (wr - wr_hi.astype(F32)).astype(BF16)], axis=-1)
    br_k = jnp.concatenate([router_group_b, router_expert_b,
                            jnp.zeros((L, LANES - N_GROUPS - N_EXPERTS), F32)], axis=-1)[:, None, :]

    for l in range(L):
        sh1, sc1, g1, sh2, sc2, g2 = [mod[l, :, i * D:(i + 1) * D][:, None, :] for i in range(6)]
        a1 = norm1_g[l][None, None, :] * (1.0 + sc1)
        a2 = norm2_g[l][None, None, :] * (1.0 + sc2)

        proj = _norm_matmul(x, a1, sh1, w_in_head, w_in_tail, l)
        q, k, v = _mla_prep(proj, qg_k, kvg_k, w_uq_k, w_ukv_k, cos_m, sin_m, l)
        y_mla = _attention(q, k, v)
        y_ret = _retention(proj, cos_r, sin_r)

        x1, h2 = _oproj(y_mla, y_ret, w_o_k, x, g1, a2, sh2, l)
        h2 = h2.reshape(T, D)
        ri, rw, cnt = _router(h2, wr_k, br_k, l)
        rw = rw.reshape(B, S, LANES)

        counts = cnt[0, N_GROUPS:N_GROUPS + N_EXPERTS].astype(I32)
        padded = ((counts + te - 1) // te) * te
        ends = jnp.cumsum(padded)
        ri = ri.astype(I32)
        eids = jnp.arange(N_EXPERTS, dtype=I32)
        experts_of = jnp.concatenate([ri[0], ri[1]])
        ranks_of = jnp.concatenate([ri[2], ri[3]])
        pos = ranks_of + jnp.sum(jnp.where(eids[:, None] < experts_of[None, :], padded[:, None], 0), axis=0)
        n_tiles = (ends[-1] // te).astype(I32)
        tile_ids = jnp.minimum(jnp.arange(n_tiles_max, dtype=I32), n_tiles - 1)
        tile_expert = jnp.sum((ends // te)[None, :] <= tile_ids[:, None], axis=1).astype(I32)
        src = _invert(pos, n_rows)
        later = jnp.logical_and(eids[None, :] > eids[:, None], (padded > 0)[None, :])
        next_of = jnp.min(jnp.where(later, eids[None, :], N_EXPERTS), axis=1)
        next_of = jnp.where(next_of < N_EXPERTS, next_of, -1)
        next_expert = jnp.sum(jnp.where(tile_expert[:, None] == eids[None, :], next_of[None, :], 0), axis=1).astype(I32)
        ys = _experts(tile_expert, next_expert, n_tiles.reshape(1), src, h2, w_gate, w_up, w_down, l, te)
        x = _combine(pos, x1, rw, g2, fg, ys, final=(l == L - 1))
    return x
```

```python
import functools

import numpy as np
import jax
import jax.numpy as jnp
from jax import lax
from jax.experimental import pallas as pl
from jax.experimental.pallas import tpu as pltpu

F32 = jnp.float32
BF16 = jnp.bfloat16
I32 = jnp.int32
EPS = 1e-6
ROPE_BASE = 10000.0
CHUNK = 64

MLA_HEADS = 8
MLA_Q_RANK = 512
MLA_KV_RANK = 256
MLA_NOPE = 128
MLA_ROPE = 64
MLA_V = 128
RET_HEADS = 8
RET_DK = 128
RET_DV = 128
N_GROUPS = 4
EXPERTS_PER_GROUP = 8
N_EXPERTS = N_GROUPS * EXPERTS_PER_GROUP
CHUNK_SHIFT = CHUNK.bit_length() - 1
EPG_SHIFT = EXPERTS_PER_GROUP.bit_length() - 1

LANES = 128
QK_WIDTH = 2 * LANES
MIX_HALF = MLA_HEADS * MLA_V
PROJ_GROUP = 1024
PROJ_WIDTH = 5 * PROJ_GROUP
NEG_BIG = -1e30
VMEM_LIMIT = 56 * 1024 * 1024


def _tile(n, pref):
    if n <= pref:
        return n
    t = pref - pref % LANES
    while n % t:
        t -= LANES
    assert t > 0, (n, pref)
    return t


def _params(sem, vmem=VMEM_LIMIT):
    return pltpu.CompilerParams(dimension_semantics=sem, vmem_limit_bytes=vmem)


def _mod_kernel(c_ref, w_ref, b_ref, o_ref):
    c = c_ref[...]
    s = (c * jax.nn.sigmoid(c)).astype(BF16)
    o_ref[...] = jnp.dot(s, w_ref[...].astype(BF16), preferred_element_type=F32) + b_ref[...]


def _adaln_mod(c, ada_w, ada_b):
    B, D = c.shape
    L, _, N = ada_w.shape
    bp = 8
    cp = jnp.pad(c, ((0, bp - B), (0, 0)))
    tn = _tile(N, 1024)
    out = pl.pallas_call(
        _mod_kernel,
        grid=(L, N // tn),
        in_specs=[
            pl.BlockSpec((bp, D), lambda l, j: (0, 0)),
            pl.BlockSpec((None, D, tn), lambda l, j: (l, 0, j)),
            pl.BlockSpec((None, 1, tn), lambda l, j: (l, 0, j)),
        ],
        out_specs=pl.BlockSpec((None, bp, tn), lambda l, j: (l, 0, j)),
        out_shape=jax.ShapeDtypeStruct((L, bp, N), F32),
        compiler_params=_params(("arbitrary", "arbitrary")),
        name="adaln_mod",
    )(cp, ada_w, ada_b.reshape(L, 1, N))
    return out[:, :B]


def _norm_matmul_kernel(x_ref, a_ref, b_ref, wa_ref, wb_ref, o_ref, h_ref, *, prenormed):
    j = pl.program_id(2)

    lhs = x_ref if prenormed else h_ref

    @pl.when(j == 0)
    def _():
        if not prenormed:
            x = x_ref[...]
            ms = jnp.mean(x * x, axis=-1, keepdims=True)
            h_ref[...] = (x * lax.rsqrt(ms + EPS) * a_ref[...] + b_ref[...]).astype(BF16)
        o_ref[...] = jnp.dot(lhs[...], wa_ref[...], preferred_element_type=F32).astype(o_ref.dtype)

    @pl.when(j > 0)
    def _():
        o_ref[...] = jnp.dot(lhs[...], wb_ref[...], preferred_element_type=F32).astype(o_ref.dtype)


def _norm_matmul(x, a, b, wa, wb, layer):
    B, S, D = x.shape
    tn = PROJ_GROUP
    nb = wb.shape[-1] // tn
    tm = _tile(S, 1024)
    return pl.pallas_call(
        functools.partial(_norm_matmul_kernel, prenormed=(x.dtype == BF16)),
        grid=(B, S // tm, 1 + nb),
        in_specs=[
            pl.BlockSpec((None, tm, D), lambda b, i, j: (b, i, 0)),
            pl.BlockSpec((None, 1, D), lambda b, i, j: (b, 0, 0)),
            pl.BlockSpec((None, 1, D), lambda b, i, j: (b, 0, 0)),
            pl.BlockSpec((None, D, tn), lambda b, i, j: (layer, 0, 0)),
            pl.BlockSpec((None, D, tn), lambda b, i, j: (layer, 0, jnp.maximum(j - 1, 0))),
        ],
        out_specs=pl.BlockSpec((None, tm, tn), lambda b, i, j: (b, i, j)),
        out_shape=jax.ShapeDtypeStruct((B, S, (1 + nb) * tn), BF16),
        scratch_shapes=[pltpu.VMEM((tm, D), BF16)],
        compiler_params=_params(("arbitrary", "arbitrary", "arbitrary")),
        name="norm_matmul",
    )(x, a, b, wa, wb)


def _rms(x, g):
    return x * lax.rsqrt(jnp.mean(x * x, axis=-1, keepdims=True) + EPS) * g


def _rope_block(x, cosb, sinb):
    return x * cosb + pltpu.roll(x, LANES // 2, 1) * sinb


def _mla_prep_kernel(p_ref, qg_ref, kvg_ref, wuq_ref, wukv_ref, cos_ref, sin_ref, q_ref, k_ref, v_ref):
    p = p_ref[...].astype(F32)
    cq = p[:, :MLA_Q_RANK]
    ckv = p[:, MLA_Q_RANK:MLA_Q_RANK + MLA_KV_RANK]
    kpe = p[:, MLA_Q_RANK + MLA_KV_RANK:MLA_Q_RANK + MLA_KV_RANK + LANES]
    cosb = cos_ref[...]
    sinb = sin_ref[...]
    scale = (MLA_NOPE + MLA_ROPE) ** -0.5 * float(np.log2(np.e))
    qf = jnp.dot(_rms(cq, qg_ref[...]).astype(BF16), wuq_ref[...], preferred_element_type=F32)
    kvf = jnp.dot(_rms(ckv, kvg_ref[...]).astype(BF16), wukv_ref[...], preferred_element_type=F32)
    kpe_r = _rope_block(kpe, cosb, sinb).astype(BF16)
    for h in range(MLA_HEADS):
        c0 = h * QK_WIDTH
        q_ref[h, :, :LANES] = (qf[:, c0:c0 + LANES] * scale).astype(BF16)
        q_ref[h, :, LANES:] = (_rope_block(qf[:, c0 + LANES:c0 + QK_WIDTH], cosb, sinb) * scale).astype(BF16)
        k_ref[h, :, :LANES] = kvf[:, c0:c0 + LANES].astype(BF16)
        k_ref[h, :, LANES:] = kpe_r
        v_ref[h] = kvf[:, c0 + LANES:c0 + QK_WIDTH].astype(BF16)


def _mla_prep(proj, qg, kvg, wuq, wukv, cosb, sinb, layer):
    B, S, _ = proj.shape
    H = MLA_HEADS
    tm = _tile(S, 512)
    return pl.pallas_call(
        _mla_prep_kernel,
        grid=(B, S // tm),
        in_specs=[
            pl.BlockSpec((None, tm, PROJ_GROUP), lambda b, i: (b, i, 0)),
            pl.BlockSpec((None, 1, MLA_Q_RANK), lambda b, i: (layer, 0, 0)),
            pl.BlockSpec((None, 1, MLA_KV_RANK), lambda b, i: (layer, 0, 0)),
            pl.BlockSpec((None, MLA_Q_RANK, H * QK_WIDTH), lambda b, i: (layer, 0, 0)),
            pl.BlockSpec((None, MLA_KV_RANK, H * QK_WIDTH), lambda b, i: (layer, 0, 0)),
            pl.BlockSpec((tm, LANES), lambda b, i: (i, 0)),
            pl.BlockSpec((tm, LANES), lambda b, i: (i, 0)),
        ],
        out_specs=[
            pl.BlockSpec((None, H, tm, QK_WIDTH), lambda b, i: (b, 0, i, 0)),
            pl.BlockSpec((None, H, tm, QK_WIDTH), lambda b, i: (b, 0, i, 0)),
            pl.BlockSpec((None, H, tm, MLA_V), lambda b, i: (b, 0, i, 0)),
        ],
        out_shape=[
            jax.ShapeDtypeStruct((B, H, S, QK_WIDTH), BF16),
            jax.ShapeDtypeStruct((B, H, S, QK_WIDTH), BF16),
            jax.ShapeDtypeStruct((B, H, S, MLA_V), BF16),
        ],
        compiler_params=_params(("arbitrary", "arbitrary")),
        name="mla_prep",
    )(proj, qg, kvg, wuq, wukv, cosb, sinb)


_NT = (((1,), (1,)), ((), ()))
_TN = (((0,), (0,)), ((), ()))


def _attn_kernel(q_ref, k_ref, v_ref, o_ref, *, tq, tk, hp):
    i = pl.program_id(2)
    sub = tq // tk

    def scores(h, j):
        start = pl.multiple_of(j * tk, tk)
        return lax.dot_general(q_ref[h], k_ref[h, pl.ds(start, tk), :], _NT, preferred_element_type=F32)

    def update(h, j, s, m, l, acc, masked_sub):
        if masked_sub is not None:
            qc = jnp.right_shift(lax.broadcasted_iota(I32, (tq, tk), 0), CHUNK_SHIFT)
            kc = jnp.right_shift(lax.broadcasted_iota(I32, (tq, tk), 1) + masked_sub * tk, CHUNK_SHIFT)
            s = jnp.where(kc <= qc, s, NEG_BIG)
        m_new = jnp.maximum(m, jnp.max(s, axis=-1, keepdims=True))
        alpha = jnp.exp2(m - m_new)
        p = jnp.exp2(s - m_new)
        l = alpha * l + jnp.sum(p, axis=-1, keepdims=True)
        start = pl.multiple_of(j * tk, tk)
        acc = alpha * acc + jnp.dot(p.astype(BF16), v_ref[h, pl.ds(start, tk), :], preferred_element_type=F32)
        return m_new, l, acc

    def body(j, carry):
        return tuple(update(h, j, scores(h, j), *carry[h], None) for h in range(hp))

    def body2(j2, carry):
        out = []
        for h in range(hp):
            c = update(h, 2 * j2, scores(h, 2 * j2), *carry[h], None)
            out.append(update(h, 2 * j2 + 1, scores(h, 2 * j2 + 1), *c, None))
        return tuple(out)

    init = tuple((jnp.full((tq, 1), NEG_BIG, F32), jnp.zeros((tq, 1), F32), jnp.zeros((tq, MLA_V), F32))
                 for h in range(hp))
    n_full = i * sub
    carry = lax.fori_loop(0, n_full // 2, body2, init)
    carry = lax.fori_loop(2 * (n_full // 2), n_full, body, carry)
    for h in range(hp):
        m, l, acc = carry[h]
        for jj in range(sub):
            m, l, acc = update(h, n_full + jj, scores(h, n_full + jj), m, l, acc, jj)
        o_ref[:, h * MLA_V:(h + 1) * MLA_V] = (acc / l).astype(o_ref.dtype)


def _attention(q, k, v):
    B, H, S, _ = q.shape
    tq = _tile(S, 512)
    tk = _tile(tq, 512)
    hp = 4
    return pl.pallas_call(
        functools.partial(_attn_kernel, tq=tq, tk=tk, hp=hp),
        grid=(B, H // hp, S // tq),
        in_specs=[
            pl.BlockSpec((None, hp, tq, QK_WIDTH), lambda b, h, i: (b, h, i, 0)),
            pl.BlockSpec((None, hp, S, QK_WIDTH), lambda b, h, i: (b, h, 0, 0)),
            pl.BlockSpec((None, hp, S, MLA_V), lambda b, h, i: (b, h, 0, 0)),
        ],
        out_specs=pl.BlockSpec((None, tq, hp * MLA_V), lambda b, h, i: (b, i, h)),
        out_shape=jax.ShapeDtypeStruct((B, S, H * MLA_V), BF16),
        compiler_params=_params(("arbitrary", "arbitrary", "arbitrary")),
        name="attention",
    )(q, k, v)


def _retention_kernel(q_ref, k_ref, v_ref, g_ref, cos_ref, sin_ref, o_ref,
                      state_ref, decay_ref, qdec_ref, kdec_ref, *, tb):
    first_block = pl.program_id(1) == 0
    log_g = [float(np.log1p(-np.exp2(-5.0 - h))) for h in range(RET_HEADS)]

    @pl.when(jnp.logical_and(pl.program_id(0) == 0, first_block))
    def _():
        rel = (lax.broadcasted_iota(I32, (tb, tb), 0) - lax.broadcasted_iota(I32, (tb, tb), 1)).astype(F32)
        idx = lax.broadcasted_iota(I32, (tb, RET_DK), 0).astype(F32)
        for h in range(RET_HEADS):
            decay_ref[h] = jnp.where(rel >= 0, jnp.exp(jnp.maximum(rel, 0.0) * log_g[h]), 0.0)
            qdec_ref[h] = jnp.exp((idx + 1.0) * log_g[h])
            kdec_ref[h] = jnp.exp((tb - 1.0 - idx) * log_g[h])

    @pl.when(first_block)
    def _():
        state_ref[...] = jnp.zeros_like(state_ref)

    cosr = cos_ref[...]
    sinr = sin_ref[...]
    for h in range(RET_HEADS):
        sl = slice(h * RET_DK, (h + 1) * RET_DK)
        q = _rope_block(q_ref[:, sl].astype(F32), cosr, sinr)
        k = _rope_block(k_ref[:, sl].astype(F32), cosr, sinr) * (RET_DK ** -0.5)
        vb = v_ref[:, sl]
        scores = lax.dot_general(q.astype(BF16), k.astype(BF16), _NT, preferred_element_type=F32) * decay_ref[h]
        y = jnp.dot(scores.astype(BF16), vb, preferred_element_type=F32)
        state = state_ref[h]
        q_dec = q * qdec_ref[h]
        y = y + jnp.dot(q_dec.astype(BF16), state.astype(BF16), preferred_element_type=F32)
        k_dec = k * kdec_ref[h]
        upd = lax.dot_general(k_dec.astype(BF16), vb, _TN, preferred_element_type=F32)
        state_ref[h] = state * float(np.exp(tb * log_g[h])) + upd
        mu = jnp.mean(y, axis=-1, keepdims=True)
        yc = y - mu
        var = jnp.mean(yc * yc, axis=-1, keepdims=True)
        g = g_ref[:, sl].astype(F32)
        o_ref[:, sl] = (g * jax.nn.sigmoid(g) * (yc * lax.rsqrt(var + EPS))).astype(o_ref.dtype)


def _retention(proj, cosr, sinr):
    B, S, _ = proj.shape
    tb = _tile(S, 256)
    W = RET_HEADS * RET_DK
    col = lambda g: pl.BlockSpec((None, tb, W), lambda b, i: (b, i, g))
    return pl.pallas_call(
        functools.partial(_retention_kernel, tb=tb),
        grid=(B, S // tb),
        in_specs=[col(1), col(2), col(3), col(4),
                  pl.BlockSpec((tb, LANES), lambda b, i: (i, 0)),
                  pl.BlockSpec((tb, LANES), lambda b, i: (i, 0))],
        out_specs=pl.BlockSpec((None, tb, W), lambda b, i: (b, i, 0)),
        out_shape=jax.ShapeDtypeStruct((B, S, W), BF16),
        scratch_shapes=[pltpu.VMEM((RET_HEADS, RET_DK, RET_DV), F32),
                        pltpu.VMEM((RET_HEADS, tb, tb), F32),
                        pltpu.VMEM((RET_HEADS, tb, RET_DK), F32),
                        pltpu.VMEM((RET_HEADS, tb, RET_DK), F32)],
        compiler_params=_params(("arbitrary", "arbitrary")),
        name="retention",
    )(proj, proj, proj, proj, cosr, sinr)


def _oproj_kernel(ym_ref, yr_ref, wo_ref, x_ref, g1_ref, a2_ref, b2_ref, x1_ref, h2_ref, *, tm, ts):
    for r0 in range(0, tm, ts):
        rows = pl.ds(r0, ts)
        acc = jnp.dot(ym_ref[rows, :], wo_ref[:MIX_HALF, :], preferred_element_type=F32)
        acc = acc + jnp.dot(yr_ref[rows, :], wo_ref[MIX_HALF:, :], preferred_element_type=F32)
        x1 = x_ref[rows, :] + g1_ref[...] * acc
        x1_ref[rows, :] = x1
        h2_ref[rows, :] = (x1 * lax.rsqrt(jnp.mean(x1 * x1, axis=-1, keepdims=True) + EPS) * a2_ref[...]
                           + b2_ref[...])


def _oproj(y_mla, y_ret, wo, x, g1, a2, b2, layer):
    B, S, D = x.shape
    tm = _tile(S, 512)
    row = lambda w: pl.BlockSpec((None, tm, w), lambda b, i: (b, i, 0))
    vec = pl.BlockSpec((None, 1, D), lambda b, i: (b, 0, 0))
    return pl.pallas_call(
        functools.partial(_oproj_kernel, tm=tm, ts=_tile(tm, 256)),
        grid=(B, S // tm),
        in_specs=[row(MIX_HALF), row(MIX_HALF),
                  pl.BlockSpec((None,) + wo.shape[1:], lambda b, i: (layer, 0, 0), pipeline_mode=pl.Buffered(1)),
                  row(D), vec, vec, vec],
        out_specs=[row(D), row(D)],
        out_shape=[jax.ShapeDtypeStruct((B, S, D), F32), jax.ShapeDtypeStruct((B, S, D), F32)],
        compiler_params=_params(("arbitrary", "arbitrary")),
        name="oproj",
    )(y_mla, y_ret, wo, x, g1, a2, b2)


ROUTE_ROWS = 8


def _router_kernel(h2_ref, wr_ref, br_ref, ri_ref, rw_ref, cnt_ref, carry_ref, *, tr):
    @pl.when(pl.program_id(0) == 0)
    def _():
        carry_ref[...] = jnp.zeros_like(carry_ref)

    h2 = h2_ref[...]
    h_hi = h2.astype(BF16)
    h_lo = (h2 - h_hi.astype(F32)).astype(BF16)
    lg2 = (jnp.dot(h_hi, wr_ref[...], preferred_element_type=F32)
           + jnp.dot(h_lo, wr_ref[...], preferred_element_type=F32))
    lg = lg2[:, :LANES] + lg2[:, LANES:] + br_ref[...]

    lane = lax.broadcasted_iota(I32, (tr, LANES), 1)
    big = jnp.int32(1 << 20)
    is_g = lane < N_GROUPS
    e_lane = lane - N_GROUPS
    is_e = jnp.logical_and(e_lane >= 0, e_lane < N_EXPERTS)
    gl = jnp.where(is_g, lg, NEG_BIG)
    gmax = jnp.max(gl, axis=-1, keepdims=True)
    gsel = jnp.min(jnp.where(gl == gmax, lane, big), axis=-1, keepdims=True)
    p_group = 1.0 / jnp.sum(jnp.where(is_g, jnp.exp(gl - gmax), 0.0), axis=-1, keepdims=True)
    in_grp = jnp.logical_and(is_e, jnp.right_shift(e_lane, EPG_SHIFT) == gsel)
    el = jnp.where(in_grp, lg, NEG_BIG)
    t1 = jnp.max(el, axis=-1, keepdims=True)
    i1 = jnp.min(jnp.where(el == t1, lane, big), axis=-1, keepdims=True)
    el2 = jnp.where(lane == i1, NEG_BIG, el)
    t2 = jnp.max(el2, axis=-1, keepdims=True)
    i2 = jnp.min(jnp.where(el2 == t2, lane, big), axis=-1, keepdims=True)
    ex = jnp.exp(t2 - t1)
    w1 = p_group / (1.0 + ex)
    w2 = p_group * ex / (1.0 + ex)

    oh1 = lane == i1
    oh2 = lane == i2
    cnt = jnp.where(jnp.logical_or(oh1, oh2), 1.0, 0.0)
    strict_lower = jnp.where(lax.broadcasted_iota(I32, (tr, tr), 1) < lax.broadcasted_iota(I32, (tr, tr), 0),
                             1.0, 0.0).astype(BF16)
    before = jnp.dot(strict_lower, cnt.astype(BF16), preferred_element_type=F32) + carry_ref[...]
    r1 = jnp.sum(jnp.where(oh1, before, 0.0), axis=-1, keepdims=True)
    r2 = jnp.sum(jnp.where(oh2, before, 0.0), axis=-1, keepdims=True)
    carry_ref[...] = carry_ref[...] + jnp.sum(cnt, axis=0, keepdims=True)
    cnt_ref[...] = carry_ref[...]

    rec = jnp.where(lane == 0, (i1 - N_GROUPS).astype(F32), jnp.where(lane == 1, (i2 - N_GROUPS).astype(F32),
                    jnp.where(lane == 2, r1, jnp.where(lane == 3, r2, 0.0))))
    ri_ref[...] = jnp.transpose(rec)[:ROUTE_ROWS, :]
    rw_ref[...] = jnp.where(lane == 0, w1, jnp.where(lane == 1, w2, 0.0))


def _router(h2, wr, br, layer):
    T, D = h2.shape
    tr = _tile(T, 1024)
    return pl.pallas_call(
        functools.partial(_router_kernel, tr=tr),
        grid=(T // tr,),
        in_specs=[pl.BlockSpec((tr, D), lambda i: (i, 0)),
                  pl.BlockSpec((None,) + wr.shape[1:], lambda i: (layer, 0, 0)),
                  pl.BlockSpec((None,) + br.shape[1:], lambda i: (layer, 0, 0))],
        out_specs=[pl.BlockSpec((ROUTE_ROWS, tr), lambda i: (0, i)),
                   pl.BlockSpec((tr, LANES), lambda i: (i, 0)),
                   pl.BlockSpec((1, LANES), lambda i: (0, 0))],
        out_shape=[jax.ShapeDtypeStruct((ROUTE_ROWS, T), F32),
                   jax.ShapeDtypeStruct((T, LANES), F32),
                   jax.ShapeDtypeStruct((1, LANES), F32)],
        scratch_shapes=[pltpu.VMEM((1, LANES), F32)],
        compiler_params=_params(("arbitrary",)),
        name="router",
    )(h2, wr, br)


def _dispatch_kernel(pos_ref, zt_ref, h_ref, xs_ref, zero_ref, sem, zsem, *, tm, n_tok, te):
    base = pl.program_id(0) * tm

    @pl.when(pl.program_id(0) == 0)
    def _():
        zero_ref[...] = jnp.zeros_like(zero_ref)

        def zero_copy(n):
            row = pl.multiple_of(zt_ref[n] * te, te)
            return pltpu.make_async_copy(zero_ref, xs_ref.at[pl.ds(row, te)], zsem)

        for n in range(zt_ref.shape[0]):
            @pl.when(zt_ref[n] >= 0)
            def _():
                zero_copy(n).start()
        for n in range(zt_ref.shape[0]):
            @pl.when(zt_ref[n] >= 0)
            def _():
                zero_copy(n).wait()

    def row_copy(r, p):
        return pltpu.make_async_copy(h_ref.at[pl.ds(r, 1)], xs_ref.at[pl.ds(p, 1)], sem)

    def issue(r, c):
        row_copy(r, pos_ref[base + r]).start()
        row_copy(r, pos_ref[n_tok + base + r]).start()
        return c

    lax.fori_loop(0, tm, issue, 0, unroll=8)
    for _ in range(2):
        pltpu.make_async_copy(h_ref, xs_ref.at[pl.ds(0, tm)], sem).wait()


def _dispatch(pos, zero_tiles, h2, n_rows, te):
    T, D = h2.shape
    tm = _tile(T, 512)
    grid_spec = pltpu.PrefetchScalarGridSpec(
        num_scalar_prefetch=2,
        grid=(T // tm,),
        in_specs=[pl.BlockSpec((tm, D), lambda i, pos, zt: (i, 0))],
        out_specs=pl.BlockSpec(memory_space=pl.ANY),
        scratch_shapes=[pltpu.VMEM((te, D), F32), pltpu.SemaphoreType.DMA(()), pltpu.SemaphoreType.DMA(())],
    )
    return pl.pallas_call(
        functools.partial(_dispatch_kernel, tm=tm, n_tok=T, te=te),
        grid_spec=grid_spec,
        out_shape=jax.ShapeDtypeStruct((n_rows, D), F32),
        compiler_params=_params(("arbitrary",)),
        name="dispatch",
    )(pos, zero_tiles, h2)


def _expert_kernel(te_ref, nx_ref, nt_ref, xs_ref, wg_hbm, wu_hbm, wd_hbm, ys_ref,
                   wgf_ref, wuf_ref, wdf_ref, wgb_ref, wub_ref, wdb_ref, sem, *, layer):
    i = pl.program_id(0)
    used = i < nt_ref[0]
    expert = te_ref[i]
    new_expert = jnp.logical_or(i == 0, expert != te_ref[jnp.maximum(i - 1, 0)])

    def fetch(e):
        return (pltpu.make_async_copy(wg_hbm.at[layer, e], wgf_ref, sem.at[0]),
                pltpu.make_async_copy(wu_hbm.at[layer, e], wuf_ref, sem.at[1]),
                pltpu.make_async_copy(wd_hbm.at[layer, e], wdf_ref, sem.at[2]))

    @pl.when(i == 0)
    def _():
        for c in fetch(expert):
            c.start()

    @pl.when(jnp.logical_and(used, new_expert))
    def _():
        for c in fetch(expert):
            c.wait()
        wgb_ref[...] = wgf_ref[...].astype(BF16)
        wub_ref[...] = wuf_ref[...].astype(BF16)
        wdb_ref[...] = wdf_ref[...].astype(BF16)

        @pl.when(nx_ref[i] >= 0)
        def _():
            for c in fetch(nx_ref[i]):
                c.start()

    @pl.when(jnp.logical_not(used))
    def _():
        ys_ref[...] = jnp.zeros_like(ys_ref)

    @pl.when(used)
    def _():
        xb = xs_ref[...].astype(BF16)
        a = jnp.dot(xb, wgb_ref[...], preferred_element_type=F32)
        u = jnp.dot(xb, wub_ref[...], preferred_element_type=F32)
        hid = (a * jax.nn.sigmoid(a) * u).astype(BF16)
        ys_ref[...] = jnp.dot(hid, wdb_ref[...], preferred_element_type=F32)


def _experts(tile_expert, next_expert, n_tiles, xs, wg, wu, wd, layer, te):
    P, D = xs.shape
    F = wg.shape[-1]
    row_idx = lambda i, te_ref, nx_ref, nt_ref: (jnp.minimum(i, nt_ref[0] - 1), 0)
    hbm = pl.BlockSpec(memory_space=pl.ANY)
    grid_spec = pltpu.PrefetchScalarGridSpec(
        num_scalar_prefetch=3,
        grid=(P // te,),
        in_specs=[pl.BlockSpec((te, D), row_idx), hbm, hbm, hbm],
        out_specs=pl.BlockSpec((te, D), lambda i, te_ref, nx_ref, nt_ref: (i, 0)),
        scratch_shapes=[pltpu.VMEM((D, F), F32), pltpu.VMEM((D, F), F32), pltpu.VMEM((F, D), F32),
                        pltpu.VMEM((D, F), BF16), pltpu.VMEM((D, F), BF16), pltpu.VMEM((F, D), BF16),
                        pltpu.SemaphoreType.DMA((3,))],
    )
    return pl.pallas_call(
        functools.partial(_expert_kernel, layer=layer),
        grid_spec=grid_spec,
        out_shape=jax.ShapeDtypeStruct((P, D), F32),
        compiler_params=_params(("arbitrary",)),
        name="experts",
    )(tile_expert, next_expert, n_tiles, xs, wg, wu, wd)


def _combine_kernel(pos_ref, x1_ref, rw_ref, g2_ref, na_ref, nb_ref, ys_ref, *rest, tm, n_tok, final):
    if final:
        o_ref, buf_ref, sem = rest
    else:
        o_ref, hn_ref, buf_ref, sem = rest
    tile = pl.program_id(0) * pl.num_programs(1) + pl.program_id(1)
    n_tile = pl.num_programs(0) * pl.num_programs(1)
    slot = lax.rem(tile, 2)

    def gather(t, s):
        base = t * tm

        def issue(r, c):
            for k in range(2):
                pltpu.make_async_copy(ys_ref.at[pl.ds(pos_ref[k * n_tok + base + r], 1)],
                                      buf_ref.at[s, k, pl.ds(r, 1)], sem.at[s]).start()
            return c

        lax.fori_loop(0, tm, issue, 0, unroll=8)

    @pl.when(tile == 0)
    def _():
        gather(tile, slot)

    @pl.when(tile + 1 < n_tile)
    def _():
        gather(tile + 1, 1 - slot)

    for k in range(2):
        pltpu.make_async_copy(ys_ref.at[pl.ds(0, tm)], buf_ref.at[slot, k], sem.at[slot]).wait()
    w = rw_ref[...]
    moe = w[:, 0:1] * buf_ref[slot, 0] + w[:, 1:2] * buf_ref[slot, 1]
    x2 = x1_ref[...] + g2_ref[...] * moe
    normed = x2 * lax.rsqrt(jnp.mean(x2 * x2, axis=-1, keepdims=True) + EPS) * na_ref[...]
    if final:
        o_ref[...] = normed
    else:
        o_ref[...] = x2
        hn_ref[...] = (normed + nb_ref[...]).astype(hn_ref.dtype)


def _combine(pos, x1, rw, g2, na, nb, ys, final):
    B, S, D = x1.shape
    tm = _tile(S, 512)
    row = pl.BlockSpec((None, tm, D), lambda b, i, pos: (b, i, 0))
    vec = pl.BlockSpec((None, 1, D), lambda b, i, pos: (b, 0, 0))
    grid_spec = pltpu.PrefetchScalarGridSpec(
        num_scalar_prefetch=1,
        grid=(B, S // tm),
        in_specs=[row, pl.BlockSpec((None, tm, LANES), lambda b, i, pos: (b, i, 0)), vec, vec, vec,
                  pl.BlockSpec(memory_space=pl.ANY)],
        out_specs=row if final else [row, row],
        scratch_shapes=[pltpu.VMEM((2, 2, tm, D), F32), pltpu.SemaphoreType.DMA((2,))],
    )
    x2_shape = jax.ShapeDtypeStruct((B, S, D), F32)
    return pl.pallas_call(
        functools.partial(_combine_kernel, tm=tm, n_tok=B * S, final=final),
        grid_spec=grid_spec,
        out_shape=x2_shape if final else [x2_shape, jax.ShapeDtypeStruct((B, S, D), BF16)],
        compiler_params=_params(("arbitrary", "arbitrary")),
        name="combine",
    )(pos, x1, rw, g2, na, nb, ys)


def _rope_tables(S, half):
    inv = ROPE_BASE ** (-jnp.arange(half, dtype=F32) / half)
    ang = jnp.arange(S, dtype=jnp.int32).astype(F32)[:, None] * inv[None, :]
    cos, sin = jnp.cos(ang), jnp.sin(ang)
    z = jnp.zeros((S, LANES // 2 - half), F32)
    return (jnp.concatenate([cos, z, cos, z], axis=-1), jnp.concatenate([-sin, z, sin, z], axis=-1))


def _layout_w_in(w_in):
    n_head = MLA_Q_RANK + MLA_KV_RANK + MLA_ROPE
    half = MLA_ROPE // 2
    z = lambda n: jnp.zeros(w_in.shape[:-1] + (n,), w_in.dtype)
    kpe = w_in[..., MLA_Q_RANK + MLA_KV_RANK:n_head]
    head = [w_in[..., :MLA_Q_RANK + MLA_KV_RANK], kpe[..., :half], z(LANES // 2 - half), kpe[..., half:],
            z(LANES // 2 - half), z(PROJ_GROUP - MLA_Q_RANK - MLA_KV_RANK - LANES)]
    return jnp.concatenate(head, axis=-1).astype(BF16), w_in[..., n_head:].astype(BF16)


def _layout_w_uq(w_uq):
    lead = w_uq.shape[:-1]
    half = MLA_ROPE // 2
    w = w_uq.reshape(lead + (MLA_HEADS, MLA_NOPE + MLA_ROPE))
    z = jnp.zeros(lead + (MLA_HEADS, LANES // 2 - half), w_uq.dtype)
    w = jnp.concatenate([w[..., :MLA_NOPE], w[..., MLA_NOPE:MLA_NOPE + half], z, w[..., MLA_NOPE + half:], z], axis=-1)
    return w.reshape(lead + (MLA_HEADS * QK_WIDTH,)).astype(BF16)


def kernel(x, c, ada_w, ada_b, norm1_g, w_in, q_norm_g, w_uq, kv_norm_g, w_ukv, w_o, norm2_g, router_group_w, router_group_b, router_expert_w, router_expert_b, w_gate, w_up, w_down, final_norm_g):
    B, S, D = x.shape
    L = ada_w.shape[0]
    T = B * S
    te = 256
    n_rows = 2 * T + N_EXPERTS * te
    n_tiles_max = n_rows // te

    mod = _adaln_mod(c, ada_w, ada_b)
    cos_m, sin_m = _rope_tables(S, MLA_ROPE // 2)
    cos_r, sin_r = _rope_tables(S, RET_DK // 2)
    fg = final_norm_g.reshape(1, D)

    w_in_head, w_in_tail = _layout_w_in(w_in)
    w_uq_k = _layout_w_uq(w_uq)
    w_ukv_k = w_ukv.astype(BF16)
    w_o_k = w_o.astype(BF16)
    qg_k = q_norm_g[:, None, :]
    kvg_k = kv_norm_g[:, None, :]
    wr = jnp.concatenate([router_group_w, router_expert_w,
                          jnp.zeros((L, D, LANES - N_GROUPS - N_EXPERTS), F32)], axis=-1)
    wr_hi = wr.astype(BF16)
    wr_k = jnp.concatenate([wr_hi, (wr - wr_hi.astype(F32)).astype(BF16)], axis=-1)
    br_k = jnp.concatenate([router_group_b, router_expert_b,
                            jnp.zeros((L, LANES - N_GROUPS - N_EXPERTS), F32)], axis=-1)[:, None, :]

    vecs = []
    for l in range(L):
        sh1, sc1, g1, sh2, sc2, g2 = [mod[l, :, i * D:(i + 1) * D][:, None, :] for i in range(6)]
        vecs.append((norm1_g[l][None, None, :] * (1.0 + sc1), sh1, g1,
                     norm2_g[l][None, None, :] * (1.0 + sc2), sh2, g2))
    final_scale = jnp.broadcast_to(fg[None], (B, 1, D))

    h_in = x
    for l in range(L):
        a1, sh1, g1, a2, sh2, g2 = vecs[l]

        proj = _norm_matmul(h_in, a1, sh1, w_in_head, w_in_tail, l)
        q, k, v = _mla_prep(proj, qg_k, kvg_k, w_uq_k, w_ukv_k, cos_m, sin_m, l)
        y_mla = _attention(q, k, v)
        y_ret = _retention(proj, cos_r, sin_r)

        x1, h2 = _oproj(y_mla, y_ret, w_o_k, x, g1, a2, sh2, l)
        h2 = h2.reshape(T, D)
        ri, rw, cnt = _router(h2, wr_k, br_k, l)
        rw = rw.reshape(B, S, LANES)

        counts = cnt[0, N_GROUPS:N_GROUPS + N_EXPERTS].astype(I32)
        padded = ((counts + te - 1) // te) * te
        ends = jnp.cumsum(padded)
        ri = ri.astype(I32)
        eids = jnp.arange(N_EXPERTS, dtype=I32)
        experts_of = jnp.concatenate([ri[0], ri[1]])
        ranks_of = jnp.concatenate([ri[2], ri[3]])
        pos = ranks_of + jnp.sum(jnp.where(eids[:, None] < experts_of[None, :], padded[:, None], 0), axis=0)
        n_tiles = (ends[-1] // te).astype(I32)
        tile_ids = jnp.minimum(jnp.arange(n_tiles_max, dtype=I32), n_tiles - 1)
        tile_expert = jnp.sum((ends // te)[None, :] <= tile_ids[:, None], axis=1).astype(I32)
        spare = n_tiles + jnp.arange(n_tiles_max - 2 * T // te, dtype=I32)
        zero_tiles = jnp.concatenate([jnp.where(padded > counts, ends // te - 1, -1),
                                      jnp.where(spare < n_tiles_max, spare, -1)]).astype(I32)

        xs = _dispatch(pos, zero_tiles, h2, n_rows, te)
        later = jnp.logical_and(eids[None, :] > eids[:, None], (padded > 0)[None, :])
        next_of = jnp.min(jnp.where(later, eids[None, :], N_EXPERTS), axis=1)
        next_of = jnp.where(next_of < N_EXPERTS, next_of, -1)
        next_expert = jnp.sum(jnp.where(tile_expert[:, None] == eids[None, :], next_of[None, :], 0), axis=1).astype(I32)
        ys = _experts(tile_expert, next_expert, n_tiles.reshape(1), xs, w_gate, w_up, w_down, l, te)
        if l == L - 1:
            return _combine(pos, x1, rw, g2, final_scale, final_scale, ys, final=True)
        x, h_in = _combine(pos, x1, rw, g2, vecs[l + 1][0], vecs[l + 1][1], ys, final=False)
```

```python
import functools

import numpy as np
import jax
import jax.numpy as jnp
from jax import lax
from jax.experimental import pallas as pl
from jax.experimental.pallas import tpu as pltpu

F32 = jnp.float32
BF16 = jnp.bfloat16
I32 = jnp.int32
EPS = 1e-6
ROPE_BASE = 10000.0
CHUNK = 64

MLA_HEADS = 8
MLA_Q_RANK = 512
MLA_KV_RANK = 256
MLA_NOPE = 128
MLA_ROPE = 64
MLA_V = 128
RET_HEADS = 8
RET_DK = 128
RET_DV = 128
N_GROUPS = 4
EXPERTS_PER_GROUP = 8
N_EXPERTS = N_GROUPS * EXPERTS_PER_GROUP
CHUNK_SHIFT = CHUNK.bit_length() - 1
EPG_SHIFT = EXPERTS_PER_GROUP.bit_length() - 1

LANES = 128
QK_WIDTH = 2 * LANES
MIX_HALF = MLA_HEADS * MLA_V
PROJ_GROUP = 1024
PROJ_WIDTH = 5 * PROJ_GROUP
NEG_BIG = -1e30
VMEM_LIMIT = 56 * 1024 * 1024


def _tile(n, pref):
    if n <= pref:
        return n
    t = pref - pref % LANES
    while n % t:
        t -= LANES
    assert t > 0, (n, pref)
    return t


def _params(sem, vmem=VMEM_LIMIT):
    return pltpu.CompilerParams(dimension_semantics=sem, vmem_limit_bytes=vmem)


def _mod_kernel(c_ref, w_ref, b_ref, o_ref):
    c = c_ref[...]
    s = (c * jax.nn.sigmoid(c)).astype(BF16)
    o_ref[...] = jnp.dot(s, w_ref[...].astype(BF16), preferred_element_type=F32) + b_ref[...]


def _adaln_mod(c, ada_w, ada_b):
    B, D = c.shape
    L, _, N = ada_w.shape
    bp = 8
    cp = jnp.pad(c, ((0, bp - B), (0, 0)))
    tn = _tile(N, 1024)
    out = pl.pallas_call(
        _mod_kernel,
        grid=(L, N // tn),
        in_specs=[
            pl.BlockSpec((bp, D), lambda l, j: (0, 0)),
            pl.BlockSpec((None, D, tn), lambda l, j: (l, 0, j)),
            pl.BlockSpec((None, 1, tn), lambda l, j: (l, 0, j)),
        ],
        out_specs=pl.BlockSpec((None, bp, tn), lambda l, j: (l, 0, j)),
        out_shape=jax.ShapeDtypeStruct((L, bp, N), F32),
        compiler_params=_params(("arbitrary", "arbitrary")),
        name="adaln_mod",
    )(cp, ada_w, ada_b.reshape(L, 1, N))
    return out[:, :B]


def _norm_matmul_kernel(x_ref, a_ref, b_ref, wa_ref, wb_ref, o_ref, h_ref):
    j = pl.program_id(2)

    @pl.when(j == 0)
    def _():
        x = x_ref[...]
        ms = jnp.mean(x * x, axis=-1, keepdims=True)
        h_ref[...] = (x * lax.rsqrt(ms + EPS) * a_ref[...] + b_ref[...]).astype(BF16)
        o_ref[...] = jnp.dot(h_ref[...], wa_ref[...], preferred_element_type=F32).astype(o_ref.dtype)

    @pl.when(j > 0)
    def _():
        o_ref[...] = jnp.dot(h_ref[...], wb_ref[...], preferred_element_type=F32).astype(o_ref.dtype)


def _norm_matmul(x, a, b, wa, wb, layer):
    B, S, D = x.shape
    tn = PROJ_GROUP
    nb = wb.shape[-1] // tn
    tm = _tile(S, 1024)
    return pl.pallas_call(
        _norm_matmul_kernel,
        grid=(B, S // tm, 1 + nb),
        in_specs=[
            pl.BlockSpec((None, tm, D), lambda b, i, j: (b, i, 0)),
            pl.BlockSpec((None, 1, D), lambda b, i, j: (b, 0, 0)),
            pl.BlockSpec((None, 1, D), lambda b, i, j: (b, 0, 0)),
            pl.BlockSpec((None, D, tn), lambda b, i, j: (layer, 0, 0)),
            pl.BlockSpec((None, D, tn), lambda b, i, j: (layer, 0, jnp.maximum(j - 1, 0))),
        ],
        out_specs=pl.BlockSpec((None, tm, tn), lambda b, i, j: (b, i, j)),
        out_shape=jax.ShapeDtypeStruct((B, S, (1 + nb) * tn), BF16),
        scratch_shapes=[pltpu.VMEM((tm, D), BF16)],
        compiler_params=_params(("arbitrary", "arbitrary", "arbitrary")),
        name="norm_matmul",
    )(x, a, b, wa, wb)


def _rms(x, g):
    return x * lax.rsqrt(jnp.mean(x * x, axis=-1, keepdims=True) + EPS) * g


def _rope_block(x, cosb, sinb):
    return x * cosb + pltpu.roll(x, LANES // 2, 1) * sinb


def _mla_prep_kernel(p_ref, qg_ref, kvg_ref, wuq_ref, wukv_ref, cos_ref, sin_ref, q_ref, k_ref, v_ref):
    p = p_ref[...].astype(F32)
    cq = p[:, :MLA_Q_RANK]
    ckv = p[:, MLA_Q_RANK:MLA_Q_RANK + MLA_KV_RANK]
    kpe = p[:, MLA_Q_RANK + MLA_KV_RANK:MLA_Q_RANK + MLA_KV_RANK + LANES]
    cosb = cos_ref[...]
    sinb = sin_ref[...]
    scale = (MLA_NOPE + MLA_ROPE) ** -0.5 * float(np.log2(np.e))
    qf = jnp.dot(_rms(cq, qg_ref[...]).astype(BF16), wuq_ref[...], preferred_element_type=F32)
    kvf = jnp.dot(_rms(ckv, kvg_ref[...]).astype(BF16), wukv_ref[...], preferred_element_type=F32)
    kpe_r = _rope_block(kpe, cosb, sinb).astype(BF16)
    for h in range(MLA_HEADS):
        c0 = h * QK_WIDTH
        q_ref[h, :, :LANES] = (qf[:, c0:c0 + LANES] * scale).astype(BF16)
        q_ref[h, :, LANES:] = (_rope_block(qf[:, c0 + LANES:c0 + QK_WIDTH], cosb, sinb) * scale).astype(BF16)
        k_ref[h, :, :LANES] = kvf[:, c0:c0 + LANES].astype(BF16)
        k_ref[h, :, LANES:] = kpe_r
        v_ref[h] = kvf[:, c0 + LANES:c0 + QK_WIDTH].astype(BF16)


def _mla_prep(proj, qg, kvg, wuq, wukv, cosb, sinb, layer):
    B, S, _ = proj.shape
    H = MLA_HEADS
    tm = _tile(S, 512)
    return pl.pallas_call(
        _mla_prep_kernel,
        grid=(B, S // tm),
        in_specs=[
            pl.BlockSpec((None, tm, PROJ_GROUP), lambda b, i: (b, i, 0)),
            pl.BlockSpec((None, 1, MLA_Q_RANK), lambda b, i: (layer, 0, 0)),
            pl.BlockSpec((None, 1, MLA_KV_RANK), lambda b, i: (layer, 0, 0)),
            pl.BlockSpec((None, MLA_Q_RANK, H * QK_WIDTH), lambda b, i: (layer, 0, 0)),
            pl.BlockSpec((None, MLA_KV_RANK, H * QK_WIDTH), lambda b, i: (layer, 0, 0)),
            pl.BlockSpec((tm, LANES), lambda b, i: (i, 0)),
            pl.BlockSpec((tm, LANES), lambda b, i: (i, 0)),
        ],
        out_specs=[
            pl.BlockSpec((None, H, tm, QK_WIDTH), lambda b, i: (b, 0, i, 0)),
            pl.BlockSpec((None, H, tm, QK_WIDTH), lambda b, i: (b, 0, i, 0)),
            pl.BlockSpec((None, H, tm, MLA_V), lambda b, i: (b, 0, i, 0)),
        ],
        out_shape=[
            jax.ShapeDtypeStruct((B, H, S, QK_WIDTH), BF16),
            jax.ShapeDtypeStruct((B, H, S, QK_WIDTH), BF16),
            jax.ShapeDtypeStruct((B, H, S, MLA_V), BF16),
        ],
        compiler_params=_params(("arbitrary", "arbitrary")),
        name="mla_prep",
    )(proj, qg, kvg, wuq, wukv, cosb, sinb)


_NT = (((1,), (1,)), ((), ()))
_TN = (((0,), (0,)), ((), ()))


def _attn_kernel(q_ref, k_ref, v_ref, o_ref, *, tq, tk, hp):
    i = pl.program_id(2)
    sub = tq // tk

    def scores(h, j):
        start = pl.multiple_of(j * tk, tk)
        return lax.dot_general(q_ref[h], k_ref[h, pl.ds(start, tk), :], _NT, preferred_element_type=F32)

    def update(h, j, s, m, l, acc, masked_sub):
        if masked_sub is not None:
            qc = jnp.right_shift(lax.broadcasted_iota(I32, (tq, tk), 0), CHUNK_SHIFT)
            kc = jnp.right_shift(lax.broadcasted_iota(I32, (tq, tk), 1) + masked_sub * tk, CHUNK_SHIFT)
            s = jnp.where(kc <= qc, s, NEG_BIG)
        m_new = jnp.maximum(m, jnp.max(s, axis=-1, keepdims=True))
        alpha = jnp.exp2(m - m_new)
        p = jnp.exp2(s - m_new)
        l = alpha * l + jnp.sum(p, axis=-1, keepdims=True)
        start = pl.multiple_of(j * tk, tk)
        acc = alpha * acc + jnp.dot(p.astype(BF16), v_ref[h, pl.ds(start, tk), :], preferred_element_type=F32)
        return m_new, l, acc

    def body(j, carry):
        return tuple(update(h, j, scores(h, j), *carry[h], None) for h in range(hp))

    def body2(j2, carry):
        out = []
        for h in range(hp):
            c = update(h, 2 * j2, scores(h, 2 * j2), *carry[h], None)
            out.append(update(h, 2 * j2 + 1, scores(h, 2 * j2 + 1), *c, None))
        return tuple(out)

    init = tuple((jnp.full((tq, 1), NEG_BIG, F32), jnp.zeros((tq, 1), F32), jnp.zeros((tq, MLA_V), F32))
                 for h in range(hp))
    n_full = i * sub
    carry = lax.fori_loop(0, n_full // 2, body2, init)
    carry = lax.fori_loop(2 * (n_full // 2), n_full, body, carry)
    for h in range(hp):
        m, l, acc = carry[h]
        for jj in range(sub):
            m, l, acc = update(h, n_full + jj, scores(h, n_full + jj), m, l, acc, jj)
        o_ref[:, h * MLA_V:(h + 1) * MLA_V] = (acc / l).astype(o_ref.dtype)


def _attention(q, k, v):
    B, H, S, _ = q.shape
    tq = _tile(S, 512)
    tk = _tile(tq, 512)
    hp = 4
    return pl.pallas_call(
        functools.partial(_attn_kernel, tq=tq, tk=tk, hp=hp),
        grid=(B, H // hp, S // tq),
        in_specs=[
            pl.BlockSpec((None, hp, tq, QK_WIDTH), lambda b, h, i: (b, h, i, 0)),
            pl.BlockSpec((None, hp, S, QK_WIDTH), lambda b, h, i: (b, h, 0, 0)),
            pl.BlockSpec((None, hp, S, MLA_V), lambda b, h, i: (b, h, 0, 0)),
        ],
        out_specs=pl.BlockSpec((None, tq, hp * MLA_V), lambda b, h, i: (b, i, h)),
        out_shape=jax.ShapeDtypeStruct((B, S, H * MLA_V), BF16),
        compiler_params=_params(("arbitrary", "arbitrary", "arbitrary")),
        name="attention",
    )(q, k, v)


def _retention_kernel(q_ref, k_ref, v_ref, g_ref, cos_ref, sin_ref, o_ref,
                      state_ref, decay_ref, qdec_ref, kdec_ref, *, tb):
    first_block = pl.program_id(1) == 0
    log_g = [float(np.log1p(-np.exp2(-5.0 - h))) for h in range(RET_HEADS)]

    @pl.when(jnp.logical_and(pl.program_id(0) == 0, first_block))
    def _():
        rel = (lax.broadcasted_iota(I32, (tb, tb), 0) - lax.broadcasted_iota(I32, (tb, tb), 1)).astype(F32)
        idx = lax.broadcasted_iota(I32, (tb, RET_DK), 0).astype(F32)
        for h in range(RET_HEADS):
            decay_ref[h] = jnp.where(rel >= 0, jnp.exp(jnp.maximum(rel, 0.0) * log_g[h]), 0.0)
            qdec_ref[h] = jnp.exp((idx + 1.0) * log_g[h])
            kdec_ref[h] = jnp.exp((tb - 1.0 - idx) * log_g[h])

    @pl.when(first_block)
    def _():
        state_ref[...] = jnp.zeros_like(state_ref)

    cosr = cos_ref[...]
    sinr = sin_ref[...]
    for h in range(RET_HEADS):
        sl = slice(h * RET_DK, (h + 1) * RET_DK)
        q = _rope_block(q_ref[:, sl].astype(F32), cosr, sinr)
        k = _rope_block(k_ref[:, sl].astype(F32), cosr, sinr) * (RET_DK ** -0.5)
        vb = v_ref[:, sl]
        scores = lax.dot_general(q.astype(BF16), k.astype(BF16), _NT, preferred_element_type=F32) * decay_ref[h]
        y = jnp.dot(scores.astype(BF16), vb, preferred_element_type=F32)
        state = state_ref[h]
        q_dec = q * qdec_ref[h]
        y = y + jnp.dot(q_dec.astype(BF16), state.astype(BF16), preferred_element_type=F32)
        k_dec = k * kdec_ref[h]
        upd = lax.dot_general(k_dec.astype(BF16), vb, _TN, preferred_element_type=F32)
        state_ref[h] = state * float(np.exp(tb * log_g[h])) + upd
        mu = jnp.mean(y, axis=-1, keepdims=True)
        yc = y - mu
        var = jnp.mean(yc * yc, axis=-1, keepdims=True)
        g = g_ref[:, sl].astype(F32)
        o_ref[:, sl] = (g * jax.nn.sigmoid(g) * (yc * lax.rsqrt(var + EPS))).astype(o_ref.dtype)


def _retention(proj, cosr, sinr):
    B, S, _ = proj.shape
    tb = _tile(S, 256)
    W = RET_HEADS * RET_DK
    col = lambda g: pl.BlockSpec((None, tb, W), lambda b, i: (b, i, g))
    return pl.pallas_call(
        functools.partial(_retention_kernel, tb=tb),
        grid=(B, S // tb),
        in_specs=[col(1), col(2), col(3), col(4),
                  pl.BlockSpec((tb, LANES), lambda b, i: (i, 0)),
                  pl.BlockSpec((tb, LANES), lambda b, i: (i, 0))],
        out_specs=pl.BlockSpec((None, tb, W), lambda b, i: (b, i, 0)),
        out_shape=jax.ShapeDtypeStruct((B, S, W), BF16),
        scratch_shapes=[pltpu.VMEM((RET_HEADS, RET_DK, RET_DV), F32),
                        pltpu.VMEM((RET_HEADS, tb, tb), F32),
                        pltpu.VMEM((RET_HEADS, tb, RET_DK), F32),
                        pltpu.VMEM((RET_HEADS, tb, RET_DK), F32)],
        compiler_params=_params(("arbitrary", "arbitrary")),
        name="retention",
    )(proj, proj, proj, proj, cosr, sinr)


def _oproj_kernel(ym_ref, yr_ref, wo_ref, x_ref, g1_ref, a2_ref, b2_ref, x1_ref, h2_ref, *, tm, ts):
    for r0 in range(0, tm, ts):
        rows = pl.ds(r0, ts)
        acc = jnp.dot(ym_ref[rows, :], wo_ref[:MIX_HALF, :], preferred_element_type=F32)
        acc = acc + jnp.dot(yr_ref[rows, :], wo_ref[MIX_HALF:, :], preferred_element_type=F32)
        x1 = x_ref[rows, :] + g1_ref[...] * acc
        x1_ref[rows, :] = x1
        h2_ref[rows, :] = (x1 * lax.rsqrt(jnp.mean(x1 * x1, axis=-1, keepdims=True) + EPS) * a2_ref[...]
                           + b2_ref[...])


def _oproj(y_mla, y_ret, wo, x, g1, a2, b2, layer):
    B, S, D = x.shape
    tm = _tile(S, 512)
    row = lambda w: pl.BlockSpec((None, tm, w), lambda b, i: (b, i, 0))
    vec = pl.BlockSpec((None, 1, D), lambda b, i: (b, 0, 0))
    return pl.pallas_call(
        functools.partial(_oproj_kernel, tm=tm, ts=_tile(tm, 256)),
        grid=(B, S // tm),
        in_specs=[row(MIX_HALF), row(MIX_HALF),
                  pl.BlockSpec((None,) + wo.shape[1:], lambda b, i: (layer, 0, 0), pipeline_mode=pl.Buffered(1)),
                  row(D), vec, vec, vec],
        out_specs=[row(D), row(D)],
        out_shape=[jax.ShapeDtypeStruct((B, S, D), F32), jax.ShapeDtypeStruct((B, S, D), F32)],
        compiler_params=_params(("arbitrary", "arbitrary")),
        name="oproj",
    )(y_mla, y_ret, wo, x, g1, a2, b2)


ROUTE_ROWS = 8


def _router_kernel(h2_ref, wr_ref, br_ref, ri_ref, rw_ref, cnt_ref, carry_ref, *, tr):
    @pl.when(pl.program_id(0) == 0)
    def _():
        carry_ref[...] = jnp.zeros_like(carry_ref)

    h2 = h2_ref[...]
    h_hi = h2.astype(BF16)
    h_lo = (h2 - h_hi.astype(F32)).astype(BF16)
    lg2 = (jnp.dot(h_hi, wr_ref[...], preferred_element_type=F32)
           + jnp.dot(h_lo, wr_ref[...], preferred_element_type=F32))
    lg = lg2[:, :LANES] + lg2[:, LANES:] + br_ref[...]

    lane = lax.broadcasted_iota(I32, (tr, LANES), 1)
    big = jnp.int32(1 << 20)
    is_g = lane < N_GROUPS
    e_lane = lane - N_GROUPS
    is_e = jnp.logical_and(e_lane >= 0, e_lane < N_EXPERTS)
    gl = jnp.where(is_g, lg, NEG_BIG)
    gmax = jnp.max(gl, axis=-1, keepdims=True)
    gsel = jnp.min(jnp.where(gl == gmax, lane, big), axis=-1, keepdims=True)
    p_group = 1.0 / jnp.sum(jnp.where(is_g, jnp.exp(gl - gmax), 0.0), axis=-1, keepdims=True)
    in_grp = jnp.logical_and(is_e, jnp.right_shift(e_lane, EPG_SHIFT) == gsel)
    el = jnp.where(in_grp, lg, NEG_BIG)
    t1 = jnp.max(el, axis=-1, keepdims=True)
    i1 = jnp.min(jnp.where(el == t1, lane, big), axis=-1, keepdims=True)
    el2 = jnp.where(lane == i1, NEG_BIG, el)
    t2 = jnp.max(el2, axis=-1, keepdims=True)
    i2 = jnp.min(jnp.where(el2 == t2, lane, big), axis=-1, keepdims=True)
    ex = jnp.exp(t2 - t1)
    w1 = p_group / (1.0 + ex)
    w2 = p_group * ex / (1.0 + ex)

    oh1 = lane == i1
    oh2 = lane == i2
    cnt = jnp.where(jnp.logical_or(oh1, oh2), 1.0, 0.0)
    strict_lower = jnp.where(lax.broadcasted_iota(I32, (tr, tr), 1) < lax.broadcasted_iota(I32, (tr, tr), 0),
                             1.0, 0.0).astype(BF16)
    before = jnp.dot(strict_lower, cnt.astype(BF16), preferred_element_type=F32) + carry_ref[...]
    r1 = jnp.sum(jnp.where(oh1, before, 0.0), axis=-1, keepdims=True)
    r2 = jnp.sum(jnp.where(oh2, before, 0.0), axis=-1, keepdims=True)
    carry_ref[...] = carry_ref[...] + jnp.sum(cnt, axis=0, keepdims=True)
    cnt_ref[...] = carry_ref[...]

    rec = jnp.where(lane == 0, (i1 - N_GROUPS).astype(F32), jnp.where(lane == 1, (i2 - N_GROUPS).astype(F32),
                    jnp.where(lane == 2, r1, jnp.where(lane == 3, r2, 0.0))))
    ri_ref[...] = jnp.transpose(rec)[:ROUTE_ROWS, :]
    rw_ref[...] = jnp.where(lane == 0, w1, jnp.where(lane == 1, w2, 0.0))


def _router(h2, wr, br, layer):
    T, D = h2.shape
    tr = _tile(T, 1024)
    return pl.pallas_call(
        functools.partial(_router_kernel, tr=tr),
        grid=(T // tr,),
        in_specs=[pl.BlockSpec((tr, D), lambda i: (i, 0)),
                  pl.BlockSpec((None,) + wr.shape[1:], lambda i: (layer, 0, 0)),
                  pl.BlockSpec((None,) + br.shape[1:], lambda i: (layer, 0, 0))],
        out_specs=[pl.BlockSpec((ROUTE_ROWS, tr), lambda i: (0, i)),
                   pl.BlockSpec((tr, LANES), lambda i: (i, 0)),
                   pl.BlockSpec((1, LANES), lambda i: (0, 0))],
        out_shape=[jax.ShapeDtypeStruct((ROUTE_ROWS, T), F32),
                   jax.ShapeDtypeStruct((T, LANES), F32),
                   jax.ShapeDtypeStruct((1, LANES), F32)],
        scratch_shapes=[pltpu.VMEM((1, LANES), F32)],
        compiler_params=_params(("arbitrary",)),
        name="router",
    )(h2, wr, br)


def _dispatch_kernel(pos_ref, zt_ref, h_ref, xs_ref, zero_ref, sem, zsem, *, tm, n_tok, te):
    base = pl.program_id(0) * tm

    @pl.when(pl.program_id(0) == 0)
    def _():
        zero_ref[...] = jnp.zeros_like(zero_ref)

        def zero_copy(n):
            row = pl.multiple_of(zt_ref[n] * te, te)
            return pltpu.make_async_copy(zero_ref, xs_ref.at[pl.ds(row, te)], zsem)

        for n in range(zt_ref.shape[0]):
            @pl.when(zt_ref[n] >= 0)
            def _():
                zero_copy(n).start()
        for n in range(zt_ref.shape[0]):
            @pl.when(zt_ref[n] >= 0)
            def _():
                zero_copy(n).wait()

    def row_copy(r, p):
        return pltpu.make_async_copy(h_ref.at[pl.ds(r, 1)], xs_ref.at[pl.ds(p, 1)], sem)

    def issue(r, c):
        row_copy(r, pos_ref[base + r]).start()
        row_copy(r, pos_ref[n_tok + base + r]).start()
        return c

    lax.fori_loop(0, tm, issue, 0, unroll=8)
    for _ in range(2):
        pltpu.make_async_copy(h_ref, xs_ref.at[pl.ds(0, tm)], sem).wait()


def _dispatch(pos, zero_tiles, h2, n_rows, te):
    T, D = h2.shape
    tm = _tile(T, 2048)
    grid_spec = pltpu.PrefetchScalarGridSpec(
        num_scalar_prefetch=2,
        grid=(T // tm,),
        in_specs=[pl.BlockSpec((tm, D), lambda i, pos, zt: (i, 0))],
        out_specs=pl.BlockSpec(memory_space=pl.ANY),
        scratch_shapes=[pltpu.VMEM((te, D), F32), pltpu.SemaphoreType.DMA(()), pltpu.SemaphoreType.DMA(())],
    )
    return pl.pallas_call(
        functools.partial(_dispatch_kernel, tm=tm, n_tok=T, te=te),
        grid_spec=grid_spec,
        out_shape=jax.ShapeDtypeStruct((n_rows, D), F32),
        compiler_params=_params(("arbitrary",)),
        name="dispatch",
    )(pos, zero_tiles, h2)


def _expert_kernel(te_ref, nx_ref, nt_ref, xs_ref, wg_hbm, wu_hbm, wd_hbm, ys_ref,
                   wgf_ref, wuf_ref, wdf_ref, wgb_ref, wub_ref, wdb_ref, sem, *, layer):
    i = pl.program_id(0)
    used = i < nt_ref[0]
    expert = te_ref[i]
    new_expert = jnp.logical_or(i == 0, expert != te_ref[jnp.maximum(i - 1, 0)])

    def fetch(e):
        return (pltpu.make_async_copy(wg_hbm.at[layer, e], wgf_ref, sem.at[0]),
                pltpu.make_async_copy(wu_hbm.at[layer, e], wuf_ref, sem.at[1]),
                pltpu.make_async_copy(wd_hbm.at[layer, e], wdf_ref, sem.at[2]))

    @pl.when(i == 0)
    def _():
        for c in fetch(expert):
            c.start()

    @pl.when(jnp.logical_and(used, new_expert))
    def _():
        for c in fetch(expert):
            c.wait()
        wgb_ref[...] = wgf_ref[...].astype(BF16)
        wub_ref[...] = wuf_ref[...].astype(BF16)
        wdb_ref[...] = wdf_ref[...].astype(BF16)

        @pl.when(nx_ref[i] >= 0)
        def _():
            for c in fetch(nx_ref[i]):
                c.start()

    @pl.when(jnp.logical_not(used))
    def _():
        ys_ref[...] = jnp.zeros_like(ys_ref)

    @pl.when(used)
    def _():
        xb = xs_ref[...].astype(BF16)
        a = jnp.dot(xb, wgb_ref[...], preferred_element_type=F32)
        u = jnp.dot(xb, wub_ref[...], preferred_element_type=F32)
        hid = (a * jax.nn.sigmoid(a) * u).astype(BF16)
        ys_ref[...] = jnp.dot(hid, wdb_ref[...], preferred_element_type=F32)


def _experts(tile_expert, next_expert, n_tiles, xs, wg, wu, wd, layer, te):
    P, D = xs.shape
    F = wg.shape[-1]
    row_idx = lambda i, te_ref, nx_ref, nt_ref: (jnp.minimum(i, nt_ref[0] - 1), 0)
    hbm = pl.BlockSpec(memory_space=pl.ANY)
    grid_spec = pltpu.PrefetchScalarGridSpec(
        num_scalar_prefetch=3,
        grid=(P // te,),
        in_specs=[pl.BlockSpec((te, D), row_idx), hbm, hbm, hbm],
        out_specs=pl.BlockSpec((te, D), lambda i, te_ref, nx_ref, nt_ref: (i, 0)),
        scratch_shapes=[pltpu.VMEM((D, F), F32), pltpu.VMEM((D, F), F32), pltpu.VMEM((F, D), F32),
                        pltpu.VMEM((D, F), BF16), pltpu.VMEM((D, F), BF16), pltpu.VMEM((F, D), BF16),
                        pltpu.SemaphoreType.DMA((3,))],
    )
    return pl.pallas_call(
        functools.partial(_expert_kernel, layer=layer),
        grid_spec=grid_spec,
        out_shape=jax.ShapeDtypeStruct((P, D), F32),
        compiler_params=_params(("arbitrary",)),
        name="experts",
    )(tile_expert, next_expert, n_tiles, xs, wg, wu, wd)


def _combine_kernel(pos_ref, x1_ref, rw_ref, g2_ref, fg_ref, ys_ref, o_ref, buf_ref, sem, *, tm, n_tok, final):
    tile = pl.program_id(0) * pl.num_programs(1) + pl.program_id(1)
    n_tile = pl.num_programs(0) * pl.num_programs(1)
    slot = lax.rem(tile, 2)

    def gather(t, s):
        base = t * tm

        def issue(r, c):
            for k in range(2):
                pltpu.make_async_copy(ys_ref.at[pl.ds(pos_ref[k * n_tok + base + r], 1)],
                                      buf_ref.at[s, k, pl.ds(r, 1)], sem.at[s]).start()
            return c

        lax.fori_loop(0, tm, issue, 0, unroll=8)

    @pl.when(tile == 0)
    def _():
        gather(tile, slot)

    @pl.when(tile + 1 < n_tile)
    def _():
        gather(tile + 1, 1 - slot)

    for k in range(2):
        pltpu.make_async_copy(ys_ref.at[pl.ds(0, tm)], buf_ref.at[slot, k], sem.at[slot]).wait()
    w = rw_ref[...]
    moe = w[:, 0:1] * buf_ref[slot, 0] + w[:, 1:2] * buf_ref[slot, 1]
    x2 = x1_ref[...] + g2_ref[...] * moe
    if final:
        x2 = x2 * lax.rsqrt(jnp.mean(x2 * x2, axis=-1, keepdims=True) + EPS) * fg_ref[...]
    o_ref[...] = x2


def _combine(pos, x1, rw, g2, fg, ys, final):
    B, S, D = x1.shape
    tm = _tile(S, 512)
    grid_spec = pltpu.PrefetchScalarGridSpec(
        num_scalar_prefetch=1,
        grid=(B, S // tm),
        in_specs=[pl.BlockSpec((None, tm, D), lambda b, i, pos: (b, i, 0)),
                  pl.BlockSpec((None, tm, LANES), lambda b, i, pos: (b, i, 0)),
                  pl.BlockSpec((None, 1, D), lambda b, i, pos: (b, 0, 0)),
                  pl.BlockSpec((1, D), lambda b, i, pos: (0, 0)),
                  pl.BlockSpec(memory_space=pl.ANY)],
        out_specs=pl.BlockSpec((None, tm, D), lambda b, i, pos: (b, i, 0)),
        scratch_shapes=[pltpu.VMEM((2, 2, tm, D), F32), pltpu.SemaphoreType.DMA((2,))],
    )
    return pl.pallas_call(
        functools.partial(_combine_kernel, tm=tm, n_tok=B * S, final=final),
        grid_spec=grid_spec,
        out_shape=jax.ShapeDtypeStruct((B, S, D), F32),
        compiler_params=_params(("arbitrary", "arbitrary")),
        name="combine",
    )(pos, x1, rw, g2, fg, ys)


def _rope_tables(S, half):
    inv = ROPE_BASE ** (-jnp.arange(half, dtype=F32) / half)
    ang = jnp.arange(S, dtype=jnp.int32).astype(F32)[:, None] * inv[None, :]
    cos, sin = jnp.cos(ang), jnp.sin(ang)
    z = jnp.zeros((S, LANES // 2 - half), F32)
    return (jnp.concatenate([cos, z, cos, z], axis=-1), jnp.concatenate([-sin, z, sin, z], axis=-1))


def _layout_w_in(w_in):
    n_head = MLA_Q_RANK + MLA_KV_RANK + MLA_ROPE
    half = MLA_ROPE // 2
    z = lambda n: jnp.zeros(w_in.shape[:-1] + (n,), w_in.dtype)
    kpe = w_in[..., MLA_Q_RANK + MLA_KV_RANK:n_head]
    head = [w_in[..., :MLA_Q_RANK + MLA_KV_RANK], kpe[..., :half], z(LANES // 2 - half), kpe[..., half:],
            z(LANES // 2 - half), z(PROJ_GROUP - MLA_Q_RANK - MLA_KV_RANK - LANES)]
    return jnp.concatenate(head, axis=-1).astype(BF16), w_in[..., n_head:].astype(BF16)


def _layout_w_uq(w_uq):
    lead = w_uq.shape[:-1]
    half = MLA_ROPE // 2
    w = w_uq.reshape(lead + (MLA_HEADS, MLA_NOPE + MLA_ROPE))
    z = jnp.zeros(lead + (MLA_HEADS, LANES // 2 - half), w_uq.dtype)
    w = jnp.concatenate([w[..., :MLA_NOPE], w[..., MLA_NOPE:MLA_NOPE + half], z, w[..., MLA_NOPE + half:], z], axis=-1)
    return w.reshape(lead + (MLA_HEADS * QK_WIDTH,)).astype(BF16)


def kernel(x, c, ada_w, ada_b, norm1_g, w_in, q_norm_g, w_uq, kv_norm_g, w_ukv, w_o, norm2_g, router_group_w, router_group_b, router_expert_w, router_expert_b, w_gate, w_up, w_down, final_norm_g):
    B, S, D = x.shape
    L = ada_w.shape[0]
    T = B * S
    te = 256
    n_rows = 2 * T + N_EXPERTS * te
    n_tiles_max = n_rows // te

    mod = _adaln_mod(c, ada_w, ada_b)
    cos_m, sin_m = _rope_tables(S, MLA_ROPE // 2)
    cos_r, sin_r = _rope_tables(S, RET_DK // 2)
    fg = final_norm_g.reshape(1, D)

    w_in_head, w_in_tail = _layout_w_in(w_in)
    w_uq_k = _layout_w_uq(w_uq)
    w_ukv_k = w_ukv.astype(BF16)
    w_o_k = w_o.astype(BF16)
    qg_k = q_norm_g[:, None, :]
    kvg_k = kv_norm_g[:, None, :]
    wr = jnp.concatenate([router_group_w, router_expert_w,
                          jnp.zeros((L, D, LANES - N_GROUPS - N_EXPERTS), F32)], axis=-1)
    wr_hi = wr.astype(BF16)
    wr_k = jnp.concatenate([wr_hi, (wr - wr_hi.astype(F32)).astype(BF16)], axis=-1)
    br_k = jnp.concatenate([router_group_b, router_expert_b,
                            jnp.zeros((L, LANES - N_GROUPS - N_EXPERTS), F32)], axis=-1)[:, None, :]

    for l in range(L):
        sh1, sc1, g1, sh2, sc2, g2 = [mod[l, :, i * D:(i + 1) * D][:, None, :] for i in range(6)]
        a1 = norm1_g[l][None, None, :] * (1.0 + sc1)
        a2 = norm2_g[l][None, None, :] * (1.0 + sc2)

        proj = _norm_matmul(x, a1, sh1, w_in_head, w_in_tail, l)
        q, k, v = _mla_prep(proj, qg_k, kvg_k, w_uq_k, w_ukv_k, cos_m, sin_m, l)
        y_mla = _attention(q, k, v)
        y_ret = _retention(proj, cos_r, sin_r)

        x1, h2 = _oproj(y_mla, y_ret, w_o_k, x, g1, a2, sh2, l)
        h2 = h2.reshape(T, D)
        ri, rw, cnt = _router(h2, wr_k, br_k, l)
        rw = rw.reshape(B, S, LANES)

        counts = cnt[0, N_GROUPS:N_GROUPS + N_EXPERTS].astype(I32)
        padded = ((counts + te - 1) // te) * te
        ends = jnp.cumsum(padded)
        ri = ri.astype(I32)
        eids = jnp.arange(N_EXPERTS, dtype=I32)
        experts_of = jnp.concatenate([ri[0], ri[1]])
        ranks_of = jnp.concatenate([ri[2], ri[3]])
        pos = ranks_of + jnp.sum(jnp.where(eids[:, None] < experts_of[None, :], padded[:, None], 0), axis=0)
        n_tiles = (ends[-1] // te).astype(I32)
        tile_ids = jnp.minimum(jnp.arange(n_tiles_max, dtype=I32), n_tiles - 1)
        tile_expert = jnp.sum((ends // te)[None, :] <= tile_ids[:, None], axis=1).astype(I32)
        spare = n_tiles + jnp.arange(n_tiles_max - 2 * T // te, dtype=I32)
        zero_tiles = jnp.concatenate([jnp.where(padded > counts, ends // te - 1, -1),
                                      jnp.where(spare < n_tiles_max, spare, -1)]).astype(I32)

        xs = _dispatch(pos, zero_tiles, h2, n_rows, te)
        later = jnp.logical_and(eids[None, :] > eids[:, None], (padded > 0)[None, :])
        next_of = jnp.min(jnp.where(later, eids[None, :], N_EXPERTS), axis=1)
        next_of = jnp.where(next_of < N_EXPERTS, next_of, -1)
        next_expert = jnp.sum(jnp.where(tile_expert[:, None] == eids[None, :], next_of[None, :], 0), axis=1).astype(I32)
        ys = _experts(tile_expert, next_expert, n_tiles.reshape(1), xs, w_gate, w_up, w_down, l, te)
        x = _combine(pos, x1, rw, g2, fg, ys, final=(l == L - 1))
    return x
```

```python
import functools

import numpy as np
import jax
import jax.numpy as jnp
from jax import lax
from jax.experimental import pallas as pl
from jax.experimental.pallas import tpu as pltpu

F32 = jnp.float32
BF16 = jnp.bfloat16
I32 = jnp.int32
EPS = 1e-6
ROPE_BASE = 10000.0
CHUNK = 64

MLA_HEADS = 8
MLA_Q_RANK = 512
MLA_KV_RANK = 256
MLA_NOPE = 128
MLA_ROPE = 64
MLA_V = 128
RET_HEADS = 8
RET_DK = 128
RET_DV = 128
N_GROUPS = 4
EXPERTS_PER_GROUP = 8
N_EXPERTS = N_GROUPS * EXPERTS_PER_GROUP
CHUNK_SHIFT = CHUNK.bit_length() - 1
EPG_SHIFT = EXPERTS_PER_GROUP.bit_length() - 1

LANES = 128
QK_WIDTH = 2 * LANES
MIX_HALF = MLA_HEADS * MLA_V
PROJ_GROUP = 1024
PROJ_WIDTH = 5 * PROJ_GROUP
NEG_BIG = -1e30
VMEM_LIMIT = 56 * 1024 * 1024


def _tile(n, pref):
    if n <= pref:
        return n
    t = pref - pref % LANES
    while n % t:
        t -= LANES
    assert t > 0, (n, pref)
    return t


def _params(sem, vmem=VMEM_LIMIT):
    return pltpu.CompilerParams(dimension_semantics=sem, vmem_limit_bytes=vmem)


def _mod_kernel(c_ref, w_ref, b_ref, o_ref):
    c = c_ref[...]
    s = (c * jax.nn.sigmoid(c)).astype(BF16)
    o_ref[...] = jnp.dot(s, w_ref[...].astype(BF16), preferred_element_type=F32) + b_ref[...]


def _adaln_mod(c, ada_w, ada_b):
    B, D = c.shape
    L, _, N = ada_w.shape
    bp = 8
    cp = jnp.pad(c, ((0, bp - B), (0, 0)))
    tn = _tile(N, 1024)
    out = pl.pallas_call(
        _mod_kernel,
        grid=(L, N // tn),
        in_specs=[
            pl.BlockSpec((bp, D), lambda l, j: (0, 0)),
            pl.BlockSpec((None, D, tn), lambda l, j: (l, 0, j)),
            pl.BlockSpec((None, 1, tn), lambda l, j: (l, 0, j)),
        ],
        out_specs=pl.BlockSpec((None, bp, tn), lambda l, j: (l, 0, j)),
        out_shape=jax.ShapeDtypeStruct((L, bp, N), F32),
        compiler_params=_params(("arbitrary", "arbitrary")),
        name="adaln_mod",
    )(cp, ada_w, ada_b.reshape(L, 1, N))
    return out[:, :B]


def _norm_matmul_kernel(x_ref, a_ref, b_ref, wa_ref, wb_ref, o_ref, h_ref):
    j = pl.program_id(2)

    @pl.when(j == 0)
    def _():
        x = x_ref[...]
        ms = jnp.mean(x * x, axis=-1, keepdims=True)
        h_ref[...] = (x * lax.rsqrt(ms + EPS) * a_ref[...] + b_ref[...]).astype(BF16)
        o_ref[...] = jnp.dot(h_ref[...], wa_ref[...], preferred_element_type=F32).astype(o_ref.dtype)

    @pl.when(j > 0)
    def _():
        o_ref[...] = jnp.dot(h_ref[...], wb_ref[...], preferred_element_type=F32).astype(o_ref.dtype)


def _norm_matmul(x, a, b, wa, wb, layer):
    B, S, D = x.shape
    tn = PROJ_GROUP
    nb = wb.shape[-1] // tn
    tm = _tile(S, 1024)
    return pl.pallas_call(
        _norm_matmul_kernel,
        grid=(B, S // tm, 1 + nb),
        in_specs=[
            pl.BlockSpec((None, tm, D), lambda b, i, j: (b, i, 0)),
            pl.BlockSpec((None, 1, D), lambda b, i, j: (b, 0, 0)),
            pl.BlockSpec((None, 1, D), lambda b, i, j: (b, 0, 0)),
            pl.BlockSpec((None, D, tn), lambda b, i, j: (layer, 0, 0)),
            pl.BlockSpec((None, D, tn), lambda b, i, j: (layer, 0, jnp.maximum(j - 1, 0))),
        ],
        out_specs=pl.BlockSpec((None, tm, tn), lambda b, i, j: (b, i, j)),
        out_shape=jax.ShapeDtypeStruct((B, S, (1 + nb) * tn), BF16),
        scratch_shapes=[pltpu.VMEM((tm, D), BF16)],
        compiler_params=_params(("arbitrary", "arbitrary", "arbitrary")),
        name="norm_matmul",
    )(x, a, b, wa, wb)


def _rms(x, g):
    return x * lax.rsqrt(jnp.mean(x * x, axis=-1, keepdims=True) + EPS) * g


def _rope_block(x, cosb, sinb):
    return x * cosb + pltpu.roll(x, LANES // 2, 1) * sinb


def _mla_prep_kernel(p_ref, qg_ref, kvg_ref, wuq_ref, wukv_ref, cos_ref, sin_ref, q_ref, k_ref, v_ref):
    p = p_ref[...].astype(F32)
    cq = p[:, :MLA_Q_RANK]
    ckv = p[:, MLA_Q_RANK:MLA_Q_RANK + MLA_KV_RANK]
    kpe = p[:, MLA_Q_RANK + MLA_KV_RANK:MLA_Q_RANK + MLA_KV_RANK + LANES]
    cosb = cos_ref[...]
    sinb = sin_ref[...]
    scale = (MLA_NOPE + MLA_ROPE) ** -0.5 * float(np.log2(np.e))
    qf = jnp.dot(_rms(cq, qg_ref[...]).astype(BF16), wuq_ref[...], preferred_element_type=F32)
    kvf = jnp.dot(_rms(ckv, kvg_ref[...]).astype(BF16), wukv_ref[...], preferred_element_type=F32)
    kpe_r = _rope_block(kpe, cosb, sinb).astype(BF16)
    for h in range(MLA_HEADS):
        c0 = h * QK_WIDTH
        q_ref[h, :, :LANES] = (qf[:, c0:c0 + LANES] * scale).astype(BF16)
        q_ref[h, :, LANES:] = (_rope_block(qf[:, c0 + LANES:c0 + QK_WIDTH], cosb, sinb) * scale).astype(BF16)
        k_ref[h, :, :LANES] = kvf[:, c0:c0 + LANES].astype(BF16)
        k_ref[h, :, LANES:] = kpe_r
        v_ref[h] = kvf[:, c0 + LANES:c0 + QK_WIDTH].astype(BF16)


def _mla_prep(proj, qg, kvg, wuq, wukv, cosb, sinb, layer):
    B, S, _ = proj.shape
    H = MLA_HEADS
    tm = _tile(S, 512)
    return pl.pallas_call(
        _mla_prep_kernel,
        grid=(B, S // tm),
        in_specs=[
            pl.BlockSpec((None, tm, PROJ_GROUP), lambda b, i: (b, i, 0)),
            pl.BlockSpec((None, 1, MLA_Q_RANK), lambda b, i: (layer, 0, 0)),
            pl.BlockSpec((None, 1, MLA_KV_RANK), lambda b, i: (layer, 0, 0)),
            pl.BlockSpec((None, MLA_Q_RANK, H * QK_WIDTH), lambda b, i: (layer, 0, 0)),
            pl.BlockSpec((None, MLA_KV_RANK, H * QK_WIDTH), lambda b, i: (layer, 0, 0)),
            pl.BlockSpec((tm, LANES), lambda b, i: (i, 0)),
            pl.BlockSpec((tm, LANES), lambda b, i: (i, 0)),
        ],
        out_specs=[
            pl.BlockSpec((None, H, tm, QK_WIDTH), lambda b, i: (b, 0, i, 0)),
            pl.BlockSpec((None, H, tm, QK_WIDTH), lambda b, i: (b, 0, i, 0)),
            pl.BlockSpec((None, H, tm, MLA_V), lambda b, i: (b, 0, i, 0)),
        ],
        out_shape=[
            jax.ShapeDtypeStruct((B, H, S, QK_WIDTH), BF16),
            jax.ShapeDtypeStruct((B, H, S, QK_WIDTH), BF16),
            jax.ShapeDtypeStruct((B, H, S, MLA_V), BF16),
        ],
        compiler_params=_params(("arbitrary", "arbitrary")),
        name="mla_prep",
    )(proj, qg, kvg, wuq, wukv, cosb, sinb)


_NT = (((1,), (1,)), ((), ()))
_TN = (((0,), (0,)), ((), ()))


def _attn_kernel(q_ref, k_ref, v_ref, o_ref, *, tq, tk, hp):
    i = pl.program_id(2)
    sub = tq // tk

    def scores(h, j):
        start = pl.multiple_of(j * tk, tk)
        return lax.dot_general(q_ref[h], k_ref[h, pl.ds(start, tk), :], _NT, preferred_element_type=F32)

    def update(h, j, s, m, l, acc, masked_sub):
        if masked_sub is not None:
            qc = jnp.right_shift(lax.broadcasted_iota(I32, (tq, tk), 0), CHUNK_SHIFT)
            kc = jnp.right_shift(lax.broadcasted_iota(I32, (tq, tk), 1) + masked_sub * tk, CHUNK_SHIFT)
            s = jnp.where(kc <= qc, s, NEG_BIG)
        m_new = jnp.maximum(m, jnp.max(s, axis=-1, keepdims=True))
        alpha = jnp.exp2(m - m_new)
        p = jnp.exp2(s - m_new)
        l = alpha * l + jnp.sum(p, axis=-1, keepdims=True)
        start = pl.multiple_of(j * tk, tk)
        acc = alpha * acc + jnp.dot(p.astype(BF16), v_ref[h, pl.ds(start, tk), :], preferred_element_type=F32)
        return m_new, l, acc

    def body(j, carry):
        return tuple(update(h, j, scores(h, j), *carry[h], None) for h in range(hp))

    def body2(j2, carry):
        out = []
        for h in range(hp):
            c = update(h, 2 * j2, scores(h, 2 * j2), *carry[h], None)
            out.append(update(h, 2 * j2 + 1, scores(h, 2 * j2 + 1), *c, None))
        return tuple(out)

    init = tuple((jnp.full((tq, 1), NEG_BIG, F32), jnp.zeros((tq, 1), F32), jnp.zeros((tq, MLA_V), F32))
                 for h in range(hp))
    n_full = i * sub
    carry = lax.fori_loop(0, n_full // 2, body2, init)
    carry = lax.fori_loop(2 * (n_full // 2), n_full, body, carry)
    for h in range(hp):
        m, l, acc = carry[h]
        for jj in range(sub):
            m, l, acc = update(h, n_full + jj, scores(h, n_full + jj), m, l, acc, jj)
        o_ref[:, h * MLA_V:(h + 1) * MLA_V] = (acc / l).astype(o_ref.dtype)


def _attention(q, k, v):
    B, H, S, _ = q.shape
    tq = _tile(S, 512)
    tk = _tile(tq, 512)
    hp = 4
    return pl.pallas_call(
        functools.partial(_attn_kernel, tq=tq, tk=tk, hp=hp),
        grid=(B, H // hp, S // tq),
        in_specs=[
            pl.BlockSpec((None, hp, tq, QK_WIDTH), lambda b, h, i: (b, h, i, 0)),
            pl.BlockSpec((None, hp, S, QK_WIDTH), lambda b, h, i: (b, h, 0, 0)),
            pl.BlockSpec((None, hp, S, MLA_V), lambda b, h, i: (b, h, 0, 0)),
        ],
        out_specs=pl.BlockSpec((None, tq, hp * MLA_V), lambda b, h, i: (b, i, h)),
        out_shape=jax.ShapeDtypeStruct((B, S, H * MLA_V), BF16),
        compiler_params=_params(("arbitrary", "arbitrary", "arbitrary")),
        name="attention",
    )(q, k, v)


def _retention_kernel(q_ref, k_ref, v_ref, g_ref, cos_ref, sin_ref, o_ref,
                      state_ref, decay_ref, qdec_ref, kdec_ref, *, tb):
    first_block = pl.program_id(1) == 0
    log_g = [float(np.log1p(-np.exp2(-5.0 - h))) for h in range(RET_HEADS)]

    @pl.when(jnp.logical_and(pl.program_id(0) == 0, first_block))
    def _():
        rel = (lax.broadcasted_iota(I32, (tb, tb), 0) - lax.broadcasted_iota(I32, (tb, tb), 1)).astype(F32)
        idx = lax.broadcasted_iota(I32, (tb, RET_DK), 0).astype(F32)
        for h in range(RET_HEADS):
            decay_ref[h] = jnp.where(rel >= 0, jnp.exp(jnp.maximum(rel, 0.0) * log_g[h]), 0.0)
            qdec_ref[h] = jnp.exp((idx + 1.0) * log_g[h])
            kdec_ref[h] = jnp.exp((tb - 1.0 - idx) * log_g[h])

    @pl.when(first_block)
    def _():
        state_ref[...] = jnp.zeros_like(state_ref)

    cosr = cos_ref[...]
    sinr = sin_ref[...]
    for h in range(RET_HEADS):
        sl = slice(h * RET_DK, (h + 1) * RET_DK)
        q = _rope_block(q_ref[:, sl].astype(F32), cosr, sinr)
        k = _rope_block(k_ref[:, sl].astype(F32), cosr, sinr) * (RET_DK ** -0.5)
        vb = v_ref[:, sl]
        scores = lax.dot_general(q.astype(BF16), k.astype(BF16), _NT, preferred_element_type=F32) * decay_ref[h]
        y = jnp.dot(scores.astype(BF16), vb, preferred_element_type=F32)
        state = state_ref[h]
        q_dec = q * qdec_ref[h]
        y = y + jnp.dot(q_dec.astype(BF16), state.astype(BF16), preferred_element_type=F32)
        k_dec = k * kdec_ref[h]
        upd = lax.dot_general(k_dec.astype(BF16), vb, _TN, preferred_element_type=F32)
        state_ref[h] = state * float(np.exp(tb * log_g[h])) + upd
        mu = jnp.mean(y, axis=-1, keepdims=True)
        yc = y - mu
        var = jnp.mean(yc * yc, axis=-1, keepdims=True)
        g = g_ref[:, sl].astype(F32)
        o_ref[:, sl] = (g * jax.nn.sigmoid(g) * (yc * lax.rsqrt(var + EPS))).astype(o_ref.dtype)


def _retention(proj, cosr, sinr):
    B, S, _ = proj.shape
    tb = _tile(S, 256)
    W = RET_HEADS * RET_DK
    col = lambda g: pl.BlockSpec((None, tb, W), lambda b, i: (b, i, g))
    return pl.pallas_call(
        functools.partial(_retention_kernel, tb=tb),
        grid=(B, S // tb),
        in_specs=[col(1), col(2), col(3), col(4),
                  pl.BlockSpec((tb, LANES), lambda b, i: (i, 0)),
                  pl.BlockSpec((tb, LANES), lambda b, i: (i, 0))],
        out_specs=pl.BlockSpec((None, tb, W), lambda b, i: (b, i, 0)),
        out_shape=jax.ShapeDtypeStruct((B, S, W), BF16),
        scratch_shapes=[pltpu.VMEM((RET_HEADS, RET_DK, RET_DV), F32),
                        pltpu.VMEM((RET_HEADS, tb, tb), F32),
                        pltpu.VMEM((RET_HEADS, tb, RET_DK), F32),
                        pltpu.VMEM((RET_HEADS, tb, RET_DK), F32)],
        compiler_params=_params(("arbitrary", "arbitrary")),
        name="retention",
    )(proj, proj, proj, proj, cosr, sinr)


def _oproj_kernel(ym_ref, yr_ref, wo_ref, x_ref, g1_ref, a2_ref, b2_ref, x1_ref, h2_ref, *, tm, ts):
    for r0 in range(0, tm, ts):
        rows = pl.ds(r0, ts)
        acc = jnp.dot(ym_ref[rows, :], wo_ref[:MIX_HALF, :], preferred_element_type=F32)
        acc = acc + jnp.dot(yr_ref[rows, :], wo_ref[MIX_HALF:, :], preferred_element_type=F32)
        x1 = x_ref[rows, :] + g1_ref[...] * acc
        x1_ref[rows, :] = x1
        h2_ref[rows, :] = (x1 * lax.rsqrt(jnp.mean(x1 * x1, axis=-1, keepdims=True) + EPS) * a2_ref[...]
                           + b2_ref[...])


def _oproj(y_mla, y_ret, wo, x, g1, a2, b2, layer):
    B, S, D = x.shape
    tm = _tile(S, 512)
    row = lambda w: pl.BlockSpec((None, tm, w), lambda b, i: (b, i, 0))
    vec = pl.BlockSpec((None, 1, D), lambda b, i: (b, 0, 0))
    return pl.pallas_call(
        functools.partial(_oproj_kernel, tm=tm, ts=_tile(tm, 256)),
        grid=(B, S // tm),
        in_specs=[row(MIX_HALF), row(MIX_HALF),
                  pl.BlockSpec((None,) + wo.shape[1:], lambda b, i: (layer, 0, 0), pipeline_mode=pl.Buffered(1)),
                  row(D), vec, vec, vec],
        out_specs=[row(D), row(D)],
        out_shape=[jax.ShapeDtypeStruct((B, S, D), F32), jax.ShapeDtypeStruct((B, S, D), F32)],
        compiler_params=_params(("arbitrary", "arbitrary")),
        name="oproj",
    )(y_mla, y_ret, wo, x, g1, a2, b2)


ROUTE_ROWS = 8


def _router_kernel(h2_ref, wr_ref, br_ref, ri_ref, rw_ref, cnt_ref, carry_ref, *, tr):
    @pl.when(pl.program_id(0) == 0)
    def _():
        carry_ref[...] = jnp.zeros_like(carry_ref)

    lg2 = jnp.dot(h2_ref[...].astype(BF16), wr_ref[...], preferred_element_type=F32)
    lg = lg2[:, :LANES] + lg2[:, LANES:] + br_ref[...]

    lane = lax.broadcasted_iota(I32, (tr, LANES), 1)
    big = jnp.int32(1 << 20)
    is_g = lane < N_GROUPS
    e_lane = lane - N_GROUPS
    is_e = jnp.logical_and(e_lane >= 0, e_lane < N_EXPERTS)
    gl = jnp.where(is_g, lg, NEG_BIG)
    gmax = jnp.max(gl, axis=-1, keepdims=True)
    gsel = jnp.min(jnp.where(gl == gmax, lane, big), axis=-1, keepdims=True)
    p_group = 1.0 / jnp.sum(jnp.where(is_g, jnp.exp(gl - gmax), 0.0), axis=-1, keepdims=True)
    in_grp = jnp.logical_and(is_e, jnp.right_shift(e_lane, EPG_SHIFT) == gsel)
    el = jnp.where(in_grp, lg, NEG_BIG)
    t1 = jnp.max(el, axis=-1, keepdims=True)
    i1 = jnp.min(jnp.where(el == t1, lane, big), axis=-1, keepdims=True)
    el2 = jnp.where(lane == i1, NEG_BIG, el)
    t2 = jnp.max(el2, axis=-1, keepdims=True)
    i2 = jnp.min(jnp.where(el2 == t2, lane, big), axis=-1, keepdims=True)
    ex = jnp.exp(t2 - t1)
    w1 = p_group / (1.0 + ex)
    w2 = p_group * ex / (1.0 + ex)

    oh1 = lane == i1
    oh2 = lane == i2
    cnt = jnp.where(jnp.logical_or(oh1, oh2), 1.0, 0.0)
    strict_lower = jnp.where(lax.broadcasted_iota(I32, (tr, tr), 1) < lax.broadcasted_iota(I32, (tr, tr), 0),
                             1.0, 0.0).astype(BF16)
    before = jnp.dot(strict_lower, cnt.astype(BF16), preferred_element_type=F32) + carry_ref[...]
    r1 = jnp.sum(jnp.where(oh1, before, 0.0), axis=-1, keepdims=True)
    r2 = jnp.sum(jnp.where(oh2, before, 0.0), axis=-1, keepdims=True)
    carry_ref[...] = carry_ref[...] + jnp.sum(cnt, axis=0, keepdims=True)
    cnt_ref[...] = carry_ref[...]

    rec = jnp.where(lane == 0, (i1 - N_GROUPS).astype(F32), jnp.where(lane == 1, (i2 - N_GROUPS).astype(F32),
                    jnp.where(lane == 2, r1, jnp.where(lane == 3, r2, 0.0))))
    ri_ref[...] = jnp.transpose(rec)[:ROUTE_ROWS, :]
    rw_ref[...] = jnp.where(lane == 0, w1, jnp.where(lane == 1, w2, 0.0))


def _router(h2, wr, br, layer):
    T, D = h2.shape
    tr = _tile(T, 1024)
    return pl.pallas_call(
        functools.partial(_router_kernel, tr=tr),
        grid=(T // tr,),
        in_specs=[pl.BlockSpec((tr, D), lambda i: (i, 0)),
                  pl.BlockSpec((None,) + wr.shape[1:], lambda i: (layer, 0, 0)),
                  pl.BlockSpec((None,) + br.shape[1:], lambda i: (layer, 0, 0))],
        out_specs=[pl.BlockSpec((ROUTE_ROWS, tr), lambda i: (0, i)),
                   pl.BlockSpec((tr, LANES), lambda i: (i, 0)),
                   pl.BlockSpec((1, LANES), lambda i: (0, 0))],
        out_shape=[jax.ShapeDtypeStruct((ROUTE_ROWS, T), F32),
                   jax.ShapeDtypeStruct((T, LANES), F32),
                   jax.ShapeDtypeStruct((1, LANES), F32)],
        scratch_shapes=[pltpu.VMEM((1, LANES), F32)],
        compiler_params=_params(("arbitrary",)),
        name="router",
    )(h2, wr, br)


def _dispatch_kernel(pos_ref, zt_ref, h_ref, xs_ref, zero_ref, sem, zsem, *, tm, n_tok, te):
    base = pl.program_id(0) * tm

    @pl.when(pl.program_id(0) == 0)
    def _():
        zero_ref[...] = jnp.zeros_like(zero_ref)

        def zero_copy(n):
            row = pl.multiple_of(zt_ref[n] * te, te)
            return pltpu.make_async_copy(zero_ref, xs_ref.at[pl.ds(row, te)], zsem)

        for n in range(zt_ref.shape[0]):
            @pl.when(zt_ref[n] >= 0)
            def _():
                zero_copy(n).start()
        for n in range(zt_ref.shape[0]):
            @pl.when(zt_ref[n] >= 0)
            def _():
                zero_copy(n).wait()

    def row_copy(r, p):
        return pltpu.make_async_copy(h_ref.at[pl.ds(r, 1)], xs_ref.at[pl.ds(p, 1)], sem)

    def issue(r, c):
        row_copy(r, pos_ref[base + r]).start()
        row_copy(r, pos_ref[n_tok + base + r]).start()
        return c

    lax.fori_loop(0, tm, issue, 0, unroll=8)
    for _ in range(2):
        pltpu.make_async_copy(h_ref, xs_ref.at[pl.ds(0, tm)], sem).wait()


def _dispatch(pos, zero_tiles, h2, n_rows, te):
    T, D = h2.shape
    tm = _tile(T, 512)
    grid_spec = pltpu.PrefetchScalarGridSpec(
        num_scalar_prefetch=2,
        grid=(T // tm,),
        in_specs=[pl.BlockSpec((tm, D), lambda i, pos, zt: (i, 0))],
        out_specs=pl.BlockSpec(memory_space=pl.ANY),
        scratch_shapes=[pltpu.VMEM((te, D), F32), pltpu.SemaphoreType.DMA(()), pltpu.SemaphoreType.DMA(())],
    )
    return pl.pallas_call(
        functools.partial(_dispatch_kernel, tm=tm, n_tok=T, te=te),
        grid_spec=grid_spec,
        out_shape=jax.ShapeDtypeStruct((n_rows, D), F32),
        compiler_params=_params(("arbitrary",)),
        name="dispatch",
    )(pos, zero_tiles, h2)


def _expert_kernel(te_ref, nx_ref, nt_ref, xs_ref, wg_hbm, wu_hbm, wd_hbm, ys_ref,
                   wgf_ref, wuf_ref, wdf_ref, wgb_ref, wub_ref, wdb_ref, sem, *, layer):
    i = pl.program_id(0)
    used = i < nt_ref[0]
    expert = te_ref[i]
    new_expert = jnp.logical_or(i == 0, expert != te_ref[jnp.maximum(i - 1, 0)])

    def fetch(e):
        return (pltpu.make_async_copy(wg_hbm.at[layer, e], wgf_ref, sem.at[0]),
                pltpu.make_async_copy(wu_hbm.at[layer, e], wuf_ref, sem.at[1]),
                pltpu.make_async_copy(wd_hbm.at[layer, e], wdf_ref, sem.at[2]))

    @pl.when(i == 0)
    def _():
        for c in fetch(expert):
            c.start()

    @pl.when(jnp.logical_and(used, new_expert))
    def _():
        for c in fetch(expert):
            c.wait()
        wgb_ref[...] = wgf_ref[...].astype(BF16)
        wub_ref[...] = wuf_ref[...].astype(BF16)
        wdb_ref[...] = wdf_ref[...].astype(BF16)

        @pl.when(nx_ref[i] >= 0)
        def _():
            for c in fetch(nx_ref[i]):
                c.start()

    @pl.when(jnp.logical_not(used))
    def _():
        ys_ref[...] = jnp.zeros_like(ys_ref)

    @pl.when(used)
    def _():
        xb = xs_ref[...].astype(BF16)
        a = jnp.dot(xb, wgb_ref[...], preferred_element_type=F32)
        u = jnp.dot(xb, wub_ref[...], preferred_element_type=F32)
        hid = (a * jax.nn.sigmoid(a) * u).astype(BF16)
        ys_ref[...] = jnp.dot(hid, wdb_ref[...], preferred_element_type=F32)


def _experts(tile_expert, next_expert, n_tiles, xs, wg, wu, wd, layer, te):
    P, D = xs.shape
    F = wg.shape[-1]
    row_idx = lambda i, te_ref, nx_ref, nt_ref: (jnp.minimum(i, nt_ref[0] - 1), 0)
    hbm = pl.BlockSpec(memory_space=pl.ANY)
    grid_spec = pltpu.PrefetchScalarGridSpec(
        num_scalar_prefetch=3,
        grid=(P // te,),
        in_specs=[pl.BlockSpec((te, D), row_idx), hbm, hbm, hbm],
        out_specs=pl.BlockSpec((te, D), lambda i, te_ref, nx_ref, nt_ref: (i, 0)),
        scratch_shapes=[pltpu.VMEM((D, F), F32), pltpu.VMEM((D, F), F32), pltpu.VMEM((F, D), F32),
                        pltpu.VMEM((D, F), BF16), pltpu.VMEM((D, F), BF16), pltpu.VMEM((F, D), BF16),
                        pltpu.SemaphoreType.DMA((3,))],
    )
    return pl.pallas_call(
        functools.partial(_expert_kernel, layer=layer),
        grid_spec=grid_spec,
        out_shape=jax.ShapeDtypeStruct((P, D), F32),
        compiler_params=_params(("arbitrary",)),
        name="experts",
    )(tile_expert, next_expert, n_tiles, xs, wg, wu, wd)


def _combine_kernel(pos_ref, x1_ref, rw_ref, g2_ref, fg_ref, ys_ref, o_ref, buf_ref, sem, *, tm, n_tok, final):
    tile = pl.program_id(0) * pl.num_programs(1) + pl.program_id(1)
    n_tile = pl.num_programs(0) * pl.num_programs(1)
    slot = lax.rem(tile, 2)

    def gather(t, s):
        base = t * tm

        def issue(r, c):
            for k in range(2):
                pltpu.make_async_copy(ys_ref.at[pl.ds(pos_ref[k * n_tok + base + r], 1)],
                                      buf_ref.at[s, k, pl.ds(r, 1)], sem.at[s]).start()
            return c

        lax.fori_loop(0, tm, issue, 0, unroll=8)

    @pl.when(tile == 0)
    def _():
        gather(tile, slot)

    @pl.when(tile + 1 < n_tile)
    def _():
        gather(tile + 1, 1 - slot)

    for k in range(2):
        pltpu.make_async_copy(ys_ref.at[pl.ds(0, tm)], buf_ref.at[slot, k], sem.at[slot]).wait()
    w = rw_ref[...]
    moe = w[:, 0:1] * buf_ref[slot, 0] + w[:, 1:2] * buf_ref[slot, 1]
    x2 = x1_ref[...] + g2_ref[...] * moe
    if final:
        x2 = x2 * lax.rsqrt(jnp.mean(x2 * x2, axis=-1, keepdims=True) + EPS) * fg_ref[...]
    o_ref[...] = x2


def _combine(pos, x1, rw, g2, fg, ys, final):
    B, S, D = x1.shape
    tm = _tile(S, 512)
    grid_spec = pltpu.PrefetchScalarGridSpec(
        num_scalar_prefetch=1,
        grid=(B, S // tm),
        in_specs=[pl.BlockSpec((None, tm, D), lambda b, i, pos: (b, i, 0)),
                  pl.BlockSpec((None, tm, LANES), lambda b, i, pos: (b, i, 0)),
                  pl.BlockSpec((None, 1, D), lambda b, i, pos: (b, 0, 0)),
                  pl.BlockSpec((1, D), lambda b, i, pos: (0, 0)),
                  pl.BlockSpec(memory_space=pl.ANY)],
        out_specs=pl.BlockSpec((None, tm, D), lambda b, i, pos: (b, i, 0)),
        scratch_shapes=[pltpu.VMEM((2, 2, tm, D), F32), pltpu.SemaphoreType.DMA((2,))],
    )
    return pl.pallas_call(
        functools.partial(_combine_kernel, tm=tm, n_tok=B * S, final=final),
        grid_spec=grid_spec,
        out_shape=jax.ShapeDtypeStruct((B, S, D), F32),
        compiler_params=_params(("arbitrary", "arbitrary")),
        name="combine",
    )(pos, x1, rw, g2, fg, ys)


def _rope_tables(S, half):
    inv = ROPE_BASE ** (-jnp.arange(half, dtype=F32) / half)
    ang = jnp.arange(S, dtype=jnp.int32).astype(F32)[:, None] * inv[None, :]
    cos, sin = jnp.cos(ang), jnp.sin(ang)
    z = jnp.zeros((S, LANES // 2 - half), F32)
    return (jnp.concatenate([cos, z, cos, z], axis=-1), jnp.concatenate([-sin, z, sin, z], axis=-1))


def _layout_w_in(w_in):
    n_head = MLA_Q_RANK + MLA_KV_RANK + MLA_ROPE
    half = MLA_ROPE // 2
    z = lambda n: jnp.zeros(w_in.shape[:-1] + (n,), w_in.dtype)
    kpe = w_in[..., MLA_Q_RANK + MLA_KV_RANK:n_head]
    head = [w_in[..., :MLA_Q_RANK + MLA_KV_RANK], kpe[..., :half], z(LANES // 2 - half), kpe[..., half:],
            z(LANES // 2 - half), z(PROJ_GROUP - MLA_Q_RANK - MLA_KV_RANK - LANES)]
    return jnp.concatenate(head, axis=-1).astype(BF16), w_in[..., n_head:].astype(BF16)


def _layout_w_uq(w_uq):
    lead = w_uq.shape[:-1]
    half = MLA_ROPE // 2
    w = w_uq.reshape(lead + (MLA_HEADS, MLA_NOPE + MLA_ROPE))
    z = jnp.zeros(lead + (MLA_HEADS, LANES // 2 - half), w_uq.dtype)
    w = jnp.concatenate([w[..., :MLA_NOPE], w[..., MLA_NOPE:MLA_NOPE + half], z, w[..., MLA_NOPE + half:], z], axis=-1)
    return w.reshape(lead + (MLA_HEADS * QK_WIDTH,)).astype(BF16)


def kernel(x, c, ada_w, ada_b, norm1_g, w_in, q_norm_g, w_uq, kv_norm_g, w_ukv, w_o, norm2_g, router_group_w, router_group_b, router_expert_w, router_expert_b, w_gate, w_up, w_down, final_norm_g):
    B, S, D = x.shape
    L = ada_w.shape[0]
    T = B * S
    te = 256
    n_rows = 2 * T + N_EXPERTS * te
    n_tiles_max = n_rows // te

    mod = _adaln_mod(c, ada_w, ada_b)
    cos_m, sin_m = _rope_tables(S, MLA_ROPE // 2)
    cos_r, sin_r = _rope_tables(S, RET_DK // 2)
    fg = final_norm_g.reshape(1, D)

    w_in_head, w_in_tail = _layout_w_in(w_in)
    w_uq_k = _layout_w_uq(w_uq)
    w_ukv_k = w_ukv.astype(BF16)
    w_o_k = w_o.astype(BF16)
    qg_k = q_norm_g[:, None, :]
    kvg_k = kv_norm_g[:, None, :]
    wr = jnp.concatenate([router_group_w, router_expert_w,
                          jnp.zeros((L, D, LANES - N_GROUPS - N_EXPERTS), F32)], axis=-1)
    wr_hi = wr.astype(BF16)
    wr_k = jnp.concatenate([wr_hi, (wr - wr_hi.astype(F32)).astype(BF16)], axis=-1)
    br_k = jnp.concatenate([router_group_b, router_expert_b,
                            jnp.zeros((L, LANES - N_GROUPS - N_EXPERTS), F32)], axis=-1)[:, None, :]

    for l in range(L):
        sh1, sc1, g1, sh2, sc2, g2 = [mod[l, :, i * D:(i + 1) * D][:, None, :] for i in range(6)]
        a1 = norm1_g[l][None, None, :] * (1.0 + sc1)
        a2 = norm2_g[l][None, None, :] * (1.0 + sc2)

        proj = _norm_matmul(x, a1, sh1, w_in_head, w_in_tail, l)
        q, k, v = _mla_prep(proj, qg_k, kvg_k, w_uq_k, w_ukv_k, cos_m, sin_m, l)
        y_mla = _attention(q, k, v)
        y_ret = _retention(proj, cos_r, sin_r)

        x1, h2 = _oproj(y_mla, y_ret, w_o_k, x, g1, a2, sh2, l)
        h2 = h2.reshape(T, D)
        ri, rw, cnt = _router(h2, wr_k, br_k, l)
        rw = rw.reshape(B, S, LANES)

        counts = cnt[0, N_GROUPS:N_GROUPS + N_EXPERTS].astype(I32)
        padded = ((counts + te - 1) // te) * te
        ends = jnp.cumsum(padded)
        ri = ri.astype(I32)
        eids = jnp.arange(N_EXPERTS, dtype=I32)
        experts_of = jnp.concatenate([ri[0], ri[1]])
        ranks_of = jnp.concatenate([ri[2], ri[3]])
        pos = ranks_of + jnp.sum(jnp.where(eids[:, None] < experts_of[None, :], padded[:, None], 0), axis=0)
        n_tiles = (ends[-1] // te).astype(I32)
        tile_ids = jnp.minimum(jnp.arange(n_tiles_max, dtype=I32), n_tiles - 1)
        tile_expert = jnp.sum((ends // te)[None, :] <= tile_ids[:, None], axis=1).astype(I32)
        spare = n_tiles + jnp.arange(n_tiles_max - 2 * T // te, dtype=I32)
        zero_tiles = jnp.concatenate([jnp.where(padded > counts, ends // te - 1, -1),
                                      jnp.where(spare < n_tiles_max, spare, -1)]).astype(I32)

        xs = _dispatch(pos, zero_tiles, h2, n_rows, te)
        later = jnp.logical_and(eids[None, :] > eids[:, None], (padded > 0)[None, :])
        next_of = jnp.min(jnp.where(later, eids[None, :], N_EXPERTS), axis=1)
        next_of = jnp.where(next_of < N_EXPERTS, next_of, -1)
        next_expert = jnp.sum(jnp.where(tile_expert[:, None] == eids[None, :], next_of[None, :], 0), axis=1).astype(I32)
        ys = _experts(tile_expert, next_expert, n_tiles.reshape(1), xs, w_gate, w_up, w_down, l, te)
        x = _combine(pos, x1, rw, g2, fg, ys, final=(l == L - 1))
    return x
```
